```python
import math
import jax, jax.numpy as jnp
from jax import lax
import numpy as np

D_MODEL = 1024
BATCH = 1
SEQ = 16384
DEPTH = 1

GRID_W = 64
WIN_H = 8
WIN_W = 16
NA_HEADS = 8
NA_HEAD_DIM = 64
NA_WIDTH = NA_HEADS * NA_HEAD_DIM
SG_GROUPS = 4
SG_GROUP_DIM = 128
SG_WIDTH = SG_GROUPS * SG_GROUP_DIM
SG_CHUNK = 128
IN_COLS = 3 * NA_WIDTH + 2 * SG_WIDTH + 2 * D_MODEL
PEER_HEADS = 8
PEER_N_KEYS = 128
PEER_N_EXPERTS = PEER_N_KEYS * PEER_N_KEYS
PEER_KEY_DIM = 256
PEER_HALF = PEER_KEY_DIM // 2
PEER_TOPK = 16
PEER_BLOCK = 128
DN_ALPHA = (2 * DEPTH) ** 0.25
DN_BETA = (8 * DEPTH) ** -0.25
LN_EPS = 1e-5

kernel_name = "hybrid_natten_sgu_peer_deepnorm"


def layer_norm(x, g, b):
    xf = x.astype(jnp.float32)
    mu = jnp.mean(xf, axis=-1, keepdims=True)
    xc = xf - mu
    var = jnp.mean(xc * xc, axis=-1, keepdims=True)
    y = xc * lax.rsqrt(var + LN_EPS)
    return (y * g.astype(jnp.float32) + b.astype(jnp.float32)).astype(x.dtype)


def neighbourhood_attention(q, k, v, rpb):
    b, s, h, dh = q.shape
    rows = s // GRID_W
    kh = min(WIN_H, rows)
    kw = WIN_W
    qg = q.reshape(b, rows, GRID_W, h, dh)
    kg = k.reshape(b, rows, GRID_W, h, dh)
    vg = v.reshape(b, rows, GRID_W, h, dh)
    cols = np.arange(GRID_W)
    col_start = np.clip(cols - kw // 2, 0, GRID_W - kw)
    col_idx = col_start[:, None] + np.arange(kw)[None, :]
    dc = col_idx - cols[:, None] + (WIN_W - 1)
    scale = dh ** -0.5

    def row_block(r):
        rs = jnp.clip(r - kh // 2, 0, rows - kh)
        q_r = lax.dynamic_index_in_dim(qg, r, axis=1, keepdims=False)
        k_r = lax.dynamic_slice_in_dim(kg, rs, kh, axis=1)[:, :, col_idx]
        v_r = lax.dynamic_slice_in_dim(vg, rs, kh, axis=1)[:, :, col_idx]
        dr = rs + jnp.arange(kh) - r + (WIN_H - 1)
        bias = rpb[:, dr[None, :, None], dc[:, None, :]]
        logits = jnp.einsum('bqhd,bkqwhd->bhqkw', q_r, k_r).astype(jnp.float32) * scale
        logits = logits + bias.astype(jnp.float32)[None]
        p = jax.nn.softmax(logits.reshape(b, h, GRID_W, kh * kw), axis=-1)
        p = p.reshape(b, h, GRID_W, kh, kw).astype(v.dtype)
        return jnp.einsum('bhqkw,bkqwhd->bqhd', p, v_r)

    out = lax.map(row_block, jnp.arange(rows))
    return out.transpose(1, 0, 2, 3, 4).reshape(b, s, h * dh)


def spatial_gating(u, v, norm_g, norm_b, w_s, b_s):
    b, s, _ = u.shape
    n = s // SG_CHUNK
    v = layer_norm(v, norm_g, norm_b)
    vc = v.reshape(b, n, SG_CHUNK, SG_GROUPS, SG_GROUP_DIM)
    mixed = jnp.einsum('gpq,bnqgc->bnpgc', w_s, vc) + b_s.T[None, None, :, :, None]
    return u * mixed.reshape(b, s, SG_WIDTH)


def peer_layer(x, wq, subkeys1, subkeys2, u_tab, v_tab):
    b, s, d = x.shape
    xb = x.reshape(b * s // PEER_BLOCK, PEER_BLOCK, d)

    def block(xt):
        q = (xt @ wq).reshape(PEER_BLOCK, PEER_HEADS, PEER_KEY_DIM)
        s1 = jnp.einsum('thd,nd->thn', q[..., :PEER_HALF], subkeys1).astype(jnp.float32)
        s2 = jnp.einsum('thd,nd->thn', q[..., PEER_HALF:], subkeys2).astype(jnp.float32)
        v1, i1 = lax.top_k(s1, PEER_TOPK)
        v2, i2 = lax.top_k(s2, PEER_TOPK)
        cand = (v1[..., :, None] + v2[..., None, :]).reshape(PEER_BLOCK, PEER_HEADS, PEER_TOPK * PEER_TOPK)
        vals, pos = lax.top_k(cand, PEER_TOPK)
        e1 = jnp.take_along_axis(i1, pos // PEER_TOPK, axis=-1)
        e2 = jnp.take_along_axis(i2, pos % PEER_TOPK, axis=-1)
        ids = e1 * PEER_N_KEYS + e2
        g = jax.nn.softmax(vals, axis=-1)
        a = jnp.einsum('td,thkd->thk', xt, u_tab[ids])
        hid = (jax.nn.gelu(a.astype(jnp.float32)) * g).astype(xt.dtype)
        return jnp.einsum('thk,thkd->td', hid, v_tab[ids])

    return lax.map(block, xb).reshape(b, s, d)


def setup_inputs(seed: int = 0) -> dict:
    key = jax.random.key(seed)
    ks = jax.random.split(key, 20)
    f32 = jnp.float32
    nrm = lambda k, shp, sc: jax.random.normal(k, shp, f32) * sc
    L = DEPTH
    return {
        "x": nrm(ks[0], (BATCH, SEQ, D_MODEL), 1.0),
        "w_in": nrm(ks[1], (L, D_MODEL, IN_COLS), D_MODEL ** -0.5),
        "na_rpb": nrm(ks[2], (L, NA_HEADS, 2 * WIN_H - 1, 2 * WIN_W - 1), 0.1),
        "sg_norm_g": 1.0 + nrm(ks[3], (L, SG_WIDTH), 0.02),
        "sg_norm_b": nrm(ks[4], (L, SG_WIDTH), 0.02),
        "sg_spatial_w": nrm(ks[5], (L, SG_GROUPS, SG_CHUNK, SG_CHUNK), SG_CHUNK ** -0.5),
        "sg_spatial_b": 1.0 + nrm(ks[6], (L, SG_GROUPS, SG_CHUNK), 0.02),
        "w_branch_na": nrm(ks[7], (L, NA_WIDTH, D_MODEL), NA_WIDTH ** -0.5),
        "w_branch_sg": nrm(ks[8], (L, SG_WIDTH, D_MODEL), SG_WIDTH ** -0.5),
        "w_out": nrm(ks[9], (L, D_MODEL, D_MODEL), DN_BETA * D_MODEL ** -0.5),
        "ln1_g": 1.0 + nrm(ks[10], (L, D_MODEL), 0.02),
        "ln1_b": nrm(ks[11], (L, D_MODEL), 0.02),
        "peer_wq": nrm(ks[12], (L, D_MODEL, PEER_HEADS * PEER_KEY_DIM), D_MODEL ** -0.5),
        "peer_subkeys1": nrm(ks[13], (L, PEER_N_KEYS, PEER_HALF), PEER_HALF ** -0.5),
        "peer_subkeys2": nrm(ks[14], (L, PEER_N_KEYS, PEER_HALF), PEER_HALF ** -0.5),
        "peer_u": nrm(ks[15], (L, PEER_N_EXPERTS, D_MODEL), D_MODEL ** -0.5),
        "peer_v": nrm(ks[16], (L, PEER_N_EXPERTS, D_MODEL), DN_BETA),
        "ln2_g": 1.0 + nrm(ks[17], (L, D_MODEL), 0.02),
        "ln2_b": nrm(ks[18], (L, D_MODEL), 0.02),
    }


def reference(x, w_in, na_rpb, sg_norm_g, sg_norm_b, sg_spatial_w, sg_spatial_b,
              w_branch_na, w_branch_sg, w_out, ln1_g, ln1_b, peer_wq, peer_subkeys1,
              peer_subkeys2, peer_u, peer_v, ln2_g, ln2_b):
    b, s, _ = x.shape
    splits = np.cumsum([NA_WIDTH, NA_WIDTH, NA_WIDTH, SG_WIDTH, SG_WIDTH, D_MODEL]).tolist()
    for l in range(DEPTH):
        proj = x @ w_in[l]
        q, k, v, su, sv, ga, gb = jnp.split(proj, splits, axis=-1)
        hd = (b, s, NA_HEADS, NA_HEAD_DIM)
        y_na = neighbourhood_attention(q.reshape(hd), k.reshape(hd), v.reshape(hd), na_rpb[l])
        y_sg = spatial_gating(jax.nn.gelu(su), jax.nn.gelu(sv), sg_norm_g[l], sg_norm_b[l],
                              sg_spatial_w[l], sg_spatial_b[l])
        merged = jax.nn.sigmoid(ga) * (y_na @ w_branch_na[l]) + jax.nn.sigmoid(gb) * (y_sg @ w_branch_sg[l])
        x = layer_norm(DN_ALPHA * x + merged @ w_out[l], ln1_g[l], ln1_b[l])
        y_peer = peer_layer(x, peer_wq[l], peer_subkeys1[l], peer_subkeys2[l], peer_u[l], peer_v[l])
        x = layer_norm(DN_ALPHA * x + y_peer, ln2_g[l], ln2_b[l])
    return x
```

```python
import functools
import math

import numpy as np
import jax
import jax.numpy as jnp
from jax import lax
from jax.experimental import pallas as pl
from jax.experimental.pallas import tpu as pltpu

F32 = jnp.float32
BF16 = jnp.bfloat16

D_MODEL = 1024
SEQ = 16384
GRID_W = 64
GRID_ROWS = SEQ // GRID_W
WIN_H = 8
WIN_W = 16
NA_HEADS = 8
NA_HEAD_DIM = 64
NA_WIDTH = NA_HEADS * NA_HEAD_DIM
SG_GROUPS = 4
SG_GROUP_DIM = 128
SG_WIDTH = SG_GROUPS * SG_GROUP_DIM
SG_CHUNK = 128
PEER_HEADS = 8
PEER_N_KEYS = 128
PEER_N_EXPERTS = PEER_N_KEYS * PEER_N_KEYS
PEER_KEY_DIM = 256
PEER_HALF = PEER_KEY_DIM // 2
PEER_TOPK = 16
DN_ALPHA = 2.0 ** 0.25
LN_EPS = 1e-5
MASK_BIAS = -1e30

LANES = 128
SUBLANES = 8

INPROJ_TM = 512
NA_ROWS = 4
MIX_TM = 256
PEER_TT = 512
PEER_ET = 1024
PEER_CHUNK_ROWS = 64
VMEM_LIMIT = 56 * 1024 * 1024


def _gelu(x):
    return jax.nn.gelu(x)


def _layer_norm_rows(x, g, b):
    mu = jnp.mean(x, axis=-1, keepdims=True)
    xc = x - mu
    var = jnp.mean(xc * xc, axis=-1, keepdims=True)
    return xc * lax.rsqrt(var + LN_EPS) * g + b


def _layer_norm_cols(z, g, b):
    mu = jnp.mean(z, axis=0, keepdims=True)
    zc = z - mu
    var = jnp.mean(zc * zc, axis=0, keepdims=True)
    return zc * lax.rsqrt(var + LN_EPS) * g + b


def _tile_lanes(p128, width):
    reps = width // LANES
    return p128 if reps == 1 else jnp.concatenate([p128] * reps, axis=1)


def _inproj_kernel(x_ref, w_ref, ng_ref, nb_ref, ws_ref, bs_ref, wsg_ref,
                   q_ref, k_ref, v_ref, ga_ref, sg_ref):
    xb = x_ref[...].astype(BF16)

    def proj(lo, hi):
        return jnp.dot(xb, w_ref[:, lo:hi], preferred_element_type=F32)

    o = 0
    q_ref[...] = (proj(o, o + NA_WIDTH) * (NA_HEAD_DIM ** -0.5)).astype(BF16)
    o += NA_WIDTH
    k_ref[...] = proj(o, o + NA_WIDTH).astype(BF16)
    o += NA_WIDTH
    v_ref[...] = proj(o, o + NA_WIDTH).astype(BF16)
    o += NA_WIDTH
    u = _gelu(proj(o, o + SG_WIDTH))
    o += SG_WIDTH
    vn = _layer_norm_rows(_gelu(proj(o, o + SG_WIDTH)), ng_ref[...], nb_ref[...]).astype(BF16)
    o += SG_WIDTH
    ga_ref[...] = jax.nn.sigmoid(proj(o, o + D_MODEL))
    o += D_MODEL
    gate_b = jax.nn.sigmoid(proj(o, o + D_MODEL))

    chunks = []
    for c in range(INPROJ_TM // SG_CHUNK):
        vc = vn[c * SG_CHUNK:(c + 1) * SG_CHUNK, :]
        groups = [
            jnp.dot(ws_ref[g], vc[:, g * SG_GROUP_DIM:(g + 1) * SG_GROUP_DIM],
                    preferred_element_type=F32)
            for g in range(SG_GROUPS)
        ]
        chunks.append(jnp.concatenate(groups, axis=1) + bs_ref[...])
    y_sg = (u * jnp.concatenate(chunks, axis=0)).astype(BF16)
    sg_ref[...] = gate_b * jnp.dot(y_sg, wsg_ref[...], preferred_element_type=F32)


def _inproj(x, w_in_b, ng, nb, ws_b, bs_full, wsg_b):
    n_steps = SEQ // INPROJ_TM
    in_cols = w_in_b.shape[1]
    tok = lambda width: pl.BlockSpec((INPROJ_TM, width), lambda i: (i, 0))
    full = lambda shape: pl.BlockSpec(shape, lambda i: (0,) * len(shape))
    return pl.pallas_call(
        _inproj_kernel,
        grid=(n_steps,),
        in_specs=[
            tok(D_MODEL),
            full((D_MODEL, in_cols)),
            full((1, SG_WIDTH)),
            full((1, SG_WIDTH)),
            full((SG_GROUPS, SG_CHUNK, SG_CHUNK)),
            full((SG_CHUNK, SG_WIDTH)),
            full((SG_WIDTH, D_MODEL)),
        ],
        out_specs=[tok(NA_WIDTH), tok(NA_WIDTH), tok(NA_WIDTH), tok(D_MODEL), tok(D_MODEL)],
        out_shape=[
            jax.ShapeDtypeStruct((SEQ, NA_WIDTH), BF16),
            jax.ShapeDtypeStruct((SEQ, NA_WIDTH), BF16),
            jax.ShapeDtypeStruct((SEQ, NA_WIDTH), BF16),
            jax.ShapeDtypeStruct((SEQ, D_MODEL), F32),
            jax.ShapeDtypeStruct((SEQ, D_MODEL), F32),
        ],
        compiler_params=pltpu.CompilerParams(
            dimension_semantics=("arbitrary",), vmem_limit_bytes=VMEM_LIMIT),
        name="inproj",
    )(x, w_in_b, ng, nb, ws_b, bs_full, wsg_b)


def _natten_kernel(q_ref, kp_ref, kc_ref, kn_ref, vp_ref, vc_ref, vn_ref, bias_ref,
                   o_ref, kwin_ref, vwin_ref):
    i = pl.program_id(0)
    last = pl.num_programs(0) - 1
    kwin_ref[0:NA_ROWS] = kp_ref[...]
    kwin_ref[NA_ROWS:2 * NA_ROWS] = kc_ref[...]
    kwin_ref[2 * NA_ROWS:3 * NA_ROWS] = kn_ref[...]
    vwin_ref[0:NA_ROWS] = vp_ref[...]
    vwin_ref[NA_ROWS:2 * NA_ROWS] = vc_ref[...]
    vwin_ref[2 * NA_ROWS:3 * NA_ROWS] = vn_ref[...]

    n_keys = WIN_H * GRID_W
    lane_head = lax.broadcasted_iota(jnp.int32, (GRID_W, NA_WIDTH), 1) // NA_HEAD_DIM
    for j in range(NA_ROWS):
        off = jnp.where(i == 0, NA_ROWS, jnp.where(i == last, 0, j))
        dneg = jnp.where(i == 0, j, jnp.where(i == last, WIN_H // 2 + j, WIN_H // 2))
        kw = kwin_ref[pl.ds(off, WIN_H)].reshape(n_keys, NA_WIDTH)
        vw = vwin_ref[pl.ds(off, WIN_H)].reshape(n_keys, NA_WIDTH)
        qj = q_ref[j]
        qm = jnp.concatenate(
            [jnp.where(lane_head == h, qj, jnp.zeros_like(qj)) for h in range(NA_HEADS)], axis=0)
        logits = lax.dot_general(qm, kw, (((1,), (1,)), ((), ())), preferred_element_type=F32)
        logits = logits + bias_ref[dneg]
        m = jnp.max(logits, axis=-1, keepdims=True)
        p = jnp.exp(logits - m)
        inv = 1.0 / jnp.sum(p, axis=-1, keepdims=True)
        r = jnp.dot(p.astype(BF16), vw, preferred_element_type=F32) * inv
        acc = jnp.zeros((GRID_W, NA_WIDTH), F32)
        for h in range(NA_HEADS):
            acc = acc + jnp.where(lane_head == h, r[h * GRID_W:(h + 1) * GRID_W, :], 0.0)
        o_ref[j] = acc.astype(BF16)


def _natten(q, k, v, bias_tab):
    n_steps = GRID_ROWS // NA_ROWS
    q3 = q.reshape(GRID_ROWS, GRID_W, NA_WIDTH)
    k3 = k.reshape(GRID_ROWS, GRID_W, NA_WIDTH)
    v3 = v.reshape(GRID_ROWS, GRID_W, NA_WIDTH)
    blk = (NA_ROWS, GRID_W, NA_WIDTH)
    cur = pl.BlockSpec(blk, lambda i: (i, 0, 0))
    prev = pl.BlockSpec(blk, lambda i: (jnp.maximum(i - 1, 0), 0, 0))
    nxt = pl.BlockSpec(blk, lambda i: (jnp.minimum(i + 1, n_steps - 1), 0, 0))
    out = pl.pallas_call(
        _natten_kernel,
        grid=(n_steps,),
        in_specs=[cur, prev, cur, nxt, prev, cur, nxt,
                  pl.BlockSpec(bias_tab.shape, lambda i: (0, 0, 0))],
        out_specs=cur,
        out_shape=jax.ShapeDtypeStruct((GRID_ROWS, GRID_W, NA_WIDTH), BF16),
        scratch_shapes=[pltpu.VMEM((3 * NA_ROWS, GRID_W, NA_WIDTH), BF16),
                        pltpu.VMEM((3 * NA_ROWS, GRID_W, NA_WIDTH), BF16)],
        compiler_params=pltpu.CompilerParams(
            dimension_semantics=("arbitrary",), vmem_limit_bytes=VMEM_LIMIT),
        name="natten",
    )(q3, k3, k3, k3, v3, v3, v3, bias_tab)
    return out.reshape(SEQ, NA_WIDTH)


def _natten_bias_table(rpb):
    cols = np.arange(GRID_W)
    col_start = np.clip(cols - WIN_W // 2, 0, GRID_W - WIN_W)
    kc = np.arange(GRID_W)
    valid = (kc[None, :] >= col_start[:, None]) & (kc[None, :] < col_start[:, None] + WIN_W)
    dc = np.clip(kc[None, :] - cols[:, None] + (WIN_W - 1), 0, 2 * WIN_W - 2)
    dr = np.arange(WIN_H)[None, :] + (WIN_H - 1) - np.arange(WIN_H)[:, None]
    g = rpb[:, dr][:, :, :, dc]
    g = jnp.transpose(g, (1, 0, 3, 2, 4))
    g = jnp.where(jnp.asarray(valid)[None, None, :, None, :], g, MASK_BIAS)
    return g.reshape(WIN_H, NA_HEADS * GRID_W, WIN_H * GRID_W).astype(F32)


def _mix_kernel(x_ref, yna_ref, ga_ref, sg_ref, wna_ref, woutT_ref, g1_ref, b1_ref,
                wqT_ref, k1_ref, k2_ref,
                x1T_ref, x1Tb_ref, s1k_ref, s2k_ref, s2_ref):
    merged = ga_ref[...] * jnp.dot(yna_ref[...], wna_ref[...], preferred_element_type=F32)
    merged = (merged + sg_ref[...]).astype(BF16)
    outT = lax.dot_general(woutT_ref[...], merged, (((1,), (1,)), ((), ())),
                           preferred_element_type=F32)
    z = DN_ALPHA * x_ref[...].T + outT
    x1T = _layer_norm_cols(z, _tile_lanes(g1_ref[...], MIX_TM), _tile_lanes(b1_ref[...], MIX_TM))
    x1T_ref[...] = x1T
    x1Tb = x1T.astype(BF16)
    x1Tb_ref[...] = x1Tb
    qpT = jnp.dot(wqT_ref[...], x1Tb, preferred_element_type=F32).astype(BF16)
    for h in range(PEER_HEADS):
        lo = h * PEER_KEY_DIM
        s1 = jnp.dot(k1_ref[...], qpT[lo:lo + PEER_HALF, :], preferred_element_type=F32)
        s2 = jnp.dot(k2_ref[...], qpT[lo + PEER_HALF:lo + PEER_KEY_DIM, :],
                     preferred_element_type=F32)
        s2_ref[h] = s2
        for slab in range(MIX_TM // LANES):
            lanes = slice(slab * LANES, (slab + 1) * LANES)
            key_rows = pl.ds(h, PEER_N_KEYS, stride=PEER_HEADS)
            s1k_ref[slab, key_rows, :] = s1[:, lanes]
            s2k_ref[slab, key_rows, :] = s2[:, lanes]


def _mix(x, y_na, gate_a, sg_part, wna_b, woutT_b, g1, b1, wqT_b, k1_b, k2_b):
    n_steps = SEQ // MIX_TM
    tok = lambda width: pl.BlockSpec((MIX_TM, width), lambda i: (i, 0))
    full = lambda shape: pl.BlockSpec(shape, lambda i: (0,) * len(shape))
    featT = pl.BlockSpec((D_MODEL, MIX_TM), lambda i: (0, i))
    score = pl.BlockSpec((PEER_HEADS, PEER_N_KEYS, MIX_TM), lambda i: (0, 0, i))
    slabs = MIX_TM // LANES
    key_major = pl.BlockSpec((slabs, PEER_N_KEYS * PEER_HEADS, LANES), lambda i: (i, 0, 0))
    key_major_shape = jax.ShapeDtypeStruct((SEQ // LANES, PEER_N_KEYS * PEER_HEADS, LANES), F32)
    return pl.pallas_call(
        _mix_kernel,
        grid=(n_steps,),
        in_specs=[
            tok(D_MODEL), tok(NA_WIDTH), tok(D_MODEL), tok(D_MODEL),
            full((NA_WIDTH, D_MODEL)), full((D_MODEL, D_MODEL)),
            full((D_MODEL, LANES)), full((D_MODEL, LANES)),
            full((PEER_HEADS * PEER_KEY_DIM, D_MODEL)),
            full((PEER_N_KEYS, PEER_HALF)), full((PEER_N_KEYS, PEER_HALF)),
        ],
        out_specs=[featT, featT, key_major, key_major, score],
        out_shape=[
            jax.ShapeDtypeStruct((D_MODEL, SEQ), F32),
            jax.ShapeDtypeStruct((D_MODEL, SEQ), BF16),
            key_major_shape,
            key_major_shape,
            jax.ShapeDtypeStruct((PEER_HEADS, PEER_N_KEYS, SEQ), F32),
        ],
        compiler_params=pltpu.CompilerParams(
            dimension_semantics=("arbitrary",), vmem_limit_bytes=VMEM_LIMIT),
        name="mix",
    )(x, y_na, gate_a, sg_part, wna_b, woutT_b, g1, b1, wqT_b, k1_b, k2_b)


def _cmpx(xs, i, j):
    hi = jnp.maximum(xs[i], xs[j])
    lo = jnp.minimum(xs[i], xs[j])
    xs[i], xs[j] = hi, lo


def _bitonic_sort_desc(xs):
    xs = list(xs)
    n = len(xs)
    size = 2
    while size <= n:
        stride = size // 2
        while stride >= 1:
            for i in range(n):
                l = i ^ stride
                if l > i:
                    if (i & size) == 0:
                        _cmpx(xs, i, l)
                    else:
                        _cmpx(xs, l, i)
            stride //= 2
        size *= 2
    return xs


def _bitonic_merge_desc(xs):
    xs = list(xs)
    n = len(xs)
    stride = n // 2
    while stride >= 1:
        for i in range(n):
            if (i & stride) == 0:
                _cmpx(xs, i, i + stride)
        stride //= 2
    return xs


def _top_merge(xs, ys):
    n = len(xs)
    return _bitonic_merge_desc([jnp.maximum(xs[i], ys[n - 1 - i]) for i in range(n)])


def _top16_of(vals):
    k = PEER_TOPK
    pad = (-len(vals)) % k
    vals = list(vals) + [jnp.full_like(vals[0], -jnp.inf)] * pad
    groups = [_bitonic_sort_desc(vals[g:g + k]) for g in range(0, len(vals), k)]
    top = groups[0]
    for grp in groups[1:]:
        top = _top_merge(top, grp)
    return top


def _route_kernel(s1k_ref, s2k_ref, th_ref, c1_ref, m2_ref):
    def key(ref, a):
        return ref[0, SUBLANES * a:SUBLANES * (a + 1), :]

    v1 = _top16_of([key(s1k_ref, a) for a in range(PEER_N_KEYS)])
    v2 = _top16_of([key(s2k_ref, b) for b in range(PEER_N_KEYS)])
    cands = [v1[i] + v2[j] for i in range(PEER_TOPK) for j in range(PEER_TOPK)
             if (i + 1) * (j + 1) <= PEER_TOPK]
    top = _top16_of(cands)
    tau = top[PEER_TOPK - 1]
    z = jnp.ones_like(tau)
    for t in top[1:]:
        z = z + jnp.exp(t - top[0])
    inv_z = 1.0 / z
    m2_ref[...] = v2[0]
    for a in range(PEER_N_KEYS):
        s1a = key(s1k_ref, a)
        th = jnp.full((SUBLANES, LANES), jnp.inf, F32)
        for j in range(PEER_TOPK):
            th = jnp.where(s1a + v2[j] >= tau, v2[j], th)
        th_ref[0, SUBLANES * a:SUBLANES * (a + 1), :] = th
        c1_ref[0, SUBLANES * a:SUBLANES * (a + 1), :] = jnp.exp(s1a - v1[0]) * inv_z


def _route(s1k, s2k):
    n_steps = SEQ // LANES
    blk = pl.BlockSpec((1, PEER_N_KEYS * PEER_HEADS, LANES), lambda i: (i, 0, 0))
    shp = jax.ShapeDtypeStruct((n_steps, PEER_N_KEYS * PEER_HEADS, LANES), F32)
    return pl.pallas_call(
        _route_kernel,
        grid=(n_steps,),
        in_specs=[blk, blk],
        out_specs=[blk, blk, pl.BlockSpec((PEER_HEADS, LANES), lambda i: (0, i))],
        out_shape=[shp, shp, jax.ShapeDtypeStruct((PEER_HEADS, SEQ), F32)],
        compiler_params=pltpu.CompilerParams(
            dimension_semantics=("arbitrary",), vmem_limit_bytes=VMEM_LIMIT),
        name="route",
    )(s1k, s2k)


def _peer_kernel(x1T_ref, x1Tb_ref, u_ref, vT_ref, s2_ref, m2_ref, th_ref, c1_ref, g2_ref, b2_ref,
                 o_ref, acc_ref, p2_ref, a_ref, w_ref):
    e = pl.program_id(1)

    @pl.when(e == 0)
    def _():
        acc_ref[...] = jnp.zeros_like(acc_ref)
        for h in range(PEER_HEADS):
            p2_ref[h] = jnp.exp(s2_ref[h] - m2_ref[h:h + 1, :])

    a_ref[...] = jnp.dot(u_ref[...], x1Tb_ref[...], preferred_element_type=F32)

    blocks_per_tile = PEER_ET // PEER_N_KEYS

    def key_block(ai, carry):
        a = e * blocks_per_tile + ai
        base = pl.multiple_of(ai * PEER_N_KEYS, PEER_N_KEYS)
        head_rows = pl.ds(pl.multiple_of(a * PEER_HEADS, PEER_HEADS), PEER_HEADS)
        for lt in range(PEER_TT // LANES):
            lanes = slice(lt * LANES, (lt + 1) * LANES)
            th_heads = th_ref[lt, head_rows, :]
            c1_heads = c1_ref[lt, head_rows, :]
            for rc in range(PEER_N_KEYS // PEER_CHUNK_ROWS):
                rows = slice(rc * PEER_CHUNK_ROWS, (rc + 1) * PEER_CHUNK_ROWS)
                g = jnp.zeros((PEER_CHUNK_ROWS, LANES), F32)
                for h in range(PEER_HEADS):
                    th = th_heads[h:h + 1, :]
                    c1 = c1_heads[h:h + 1, :]
                    g = g + jnp.where(s2_ref[h, rows, lanes] >= th, p2_ref[h, rows, lanes] * c1, 0.0)
                r0 = base + rc * PEER_CHUNK_ROWS
                act = a_ref[pl.ds(r0, PEER_CHUNK_ROWS), lanes]
                w_ref[pl.ds(r0, PEER_CHUNK_ROWS), lanes] = (_gelu(act) * g).astype(BF16)
        return carry

    lax.fori_loop(0, blocks_per_tile, key_block, 0)
    acc_ref[...] += jnp.dot(vT_ref[...], w_ref[...], preferred_element_type=F32)

    @pl.when(e == pl.num_programs(1) - 1)
    def _():
        z = DN_ALPHA * x1T_ref[...] + acc_ref[...]
        y = _layer_norm_cols(z, _tile_lanes(g2_ref[...], PEER_TT), _tile_lanes(b2_ref[...], PEER_TT))
        o_ref[...] = y.T


def _peer(x1T, x1Tb, u_b, vT_b, s2T, m2, theta, c1, g2, b2):
    n_tok = SEQ // PEER_TT
    n_exp = PEER_N_EXPERTS // PEER_ET
    featT = pl.BlockSpec((D_MODEL, PEER_TT), lambda j, e: (0, j))
    side = pl.BlockSpec((PEER_HEADS, PEER_N_KEYS, PEER_TT), lambda j, e: (0, 0, j))
    key_major = pl.BlockSpec((PEER_TT // LANES, PEER_N_KEYS * PEER_HEADS, LANES),
                             lambda j, e: (j, 0, 0))
    par = pl.BlockSpec((D_MODEL, LANES), lambda j, e: (0, 0))
    return pl.pallas_call(
        _peer_kernel,
        grid=(n_tok, n_exp),
        in_specs=[
            featT, featT,
            pl.BlockSpec((PEER_ET, D_MODEL), lambda j, e: (e, 0)),
            pl.BlockSpec((D_MODEL, PEER_ET), lambda j, e: (0, e)),
            side,
            pl.BlockSpec((PEER_HEADS, PEER_TT), lambda j, e: (0, j)),
            key_major, key_major, par, par,
        ],
        out_specs=pl.BlockSpec((PEER_TT, D_MODEL), lambda j, e: (j, 0)),
        out_shape=jax.ShapeDtypeStruct((SEQ, D_MODEL), F32),
        scratch_shapes=[
            pltpu.VMEM((D_MODEL, PEER_TT), F32),
            pltpu.VMEM((PEER_HEADS, PEER_N_KEYS, PEER_TT), F32),
            pltpu.VMEM((PEER_ET, PEER_TT), F32),
            pltpu.VMEM((PEER_ET, PEER_TT), BF16),
        ],
        compiler_params=pltpu.CompilerParams(
            dimension_semantics=("arbitrary", "arbitrary"), vmem_limit_bytes=VMEM_LIMIT),
        name="peer",
    )(x1T, x1Tb, u_b, vT_b, s2T, m2, theta, c1, g2, b2)


def _lane_replicated(p):
    return jnp.broadcast_to(p.astype(F32)[:, None], (p.shape[0], LANES))


def kernel(x, w_in, na_rpb, sg_norm_g, sg_norm_b, sg_spatial_w, sg_spatial_b, w_branch_na,
           w_branch_sg, w_out, ln1_g, ln1_b, peer_wq, peer_subkeys1, peer_subkeys2, peer_u, peer_v,
           ln2_g, ln2_b):
    assert x.shape == (1, SEQ, D_MODEL) and w_in.shape[0] == 1
    l = 0
    xs = x[0]
    q, k, v, gate_a, sg_part = _inproj(
        xs,
        w_in[l].astype(BF16),
        sg_norm_g[l][None, :], sg_norm_b[l][None, :],
        sg_spatial_w[l].astype(BF16),
        jnp.repeat(sg_spatial_b[l].T, SG_GROUP_DIM, axis=1),
        w_branch_sg[l].astype(BF16),
    )
    y_na = _natten(q, k, v, _natten_bias_table(na_rpb[l]))
    x1T, x1Tb, s1k, s2k, s2T = _mix(
        xs, y_na, gate_a, sg_part,
        w_branch_na[l].astype(BF16),
        w_out[l].T.astype(BF16),
        _lane_replicated(ln1_g[l]), _lane_replicated(ln1_b[l]),
        peer_wq[l].T.astype(BF16),
        peer_subkeys1[l].astype(BF16), peer_subkeys2[l].astype(BF16),
    )
    theta, c1, m2 = _route(s1k, s2k)
    out = _peer(
        x1T, x1Tb,
        peer_u[l].astype(BF16), peer_v[l].T.astype(BF16),
        s2T, m2, theta, c1,
        _lane_replicated(ln2_g[l]), _lane_replicated(ln2_b[l]),
    )
    return out[None]
```

```python
import functools
import math

import numpy as np
import jax
import jax.numpy as jnp
from jax import lax
from jax.experimental import pallas as pl
from jax.experimental.pallas import tpu as pltpu

F32 = jnp.float32
BF16 = jnp.bfloat16

D_MODEL = 1024
SEQ = 16384
GRID_W = 64
GRID_ROWS = SEQ // GRID_W
WIN_H = 8
WIN_W = 16
NA_HEADS = 8
NA_HEAD_DIM = 64
NA_WIDTH = NA_HEADS * NA_HEAD_DIM
SG_GROUPS = 4
SG_GROUP_DIM = 128
SG_WIDTH = SG_GROUPS * SG_GROUP_DIM
SG_CHUNK = 128
PEER_HEADS = 8
PEER_N_KEYS = 128
PEER_N_EXPERTS = PEER_N_KEYS * PEER_N_KEYS
PEER_KEY_DIM = 256
PEER_HALF = PEER_KEY_DIM // 2
PEER_TOPK = 16
DN_ALPHA = 2.0 ** 0.25
LN_EPS = 1e-5
MASK_BIAS = -1e30
GELU_C0 = math.sqrt(2.0 / math.pi)
GELU_C1 = 0.044715 * GELU_C0

LANES = 128
SUBLANES = 8

INPROJ_TM = 512
NA_ROWS = 4
MIX_TM = 256
PEER_TT = 512
PEER_ET = 2048
PEER_GROUP = 512
PEER_CHUNK_ROWS = 128
VMEM_LIMIT = 56 * 1024 * 1024


def _gelu(x):
    return jax.nn.gelu(x)


def _gelu_times(x, g):
    u = x * (GELU_C0 + GELU_C1 * (x * x))
    t = jnp.tanh(u.astype(BF16))
    return (0.5 * x.astype(BF16)) * (1.0 + t) * g


def _layer_norm_rows(x, g, b):
    mu = jnp.mean(x, axis=-1, keepdims=True)
    xc = x - mu
    var = jnp.mean(xc * xc, axis=-1, keepdims=True)
    return xc * lax.rsqrt(var + LN_EPS) * g + b


def _layer_norm_cols(z, g, b):
    mu = jnp.mean(z, axis=0, keepdims=True)
    zc = z - mu
    var = jnp.mean(zc * zc, axis=0, keepdims=True)
    return zc * lax.rsqrt(var + LN_EPS) * g + b


def _tile_lanes(p128, width):
    reps = width // LANES
    return p128 if reps == 1 else jnp.concatenate([p128] * reps, axis=1)


def _inproj_kernel(x_ref, w_ref, ng_ref, nb_ref, ws_ref, bs_ref, wsg_ref,
                   q_ref, k_ref, v_ref, ga_ref, sg_ref):
    xb = x_ref[...].astype(BF16)

    def proj(lo, hi):
        return jnp.dot(xb, w_ref[:, lo:hi], preferred_element_type=F32)

    o = 0
    q_ref[...] = (proj(o, o + NA_WIDTH) * (NA_HEAD_DIM ** -0.5)).astype(BF16)
    o += NA_WIDTH
    k_ref[...] = proj(o, o + NA_WIDTH).astype(BF16)
    o += NA_WIDTH
    v_ref[...] = proj(o, o + NA_WIDTH).astype(BF16)
    o += NA_WIDTH
    u = _gelu(proj(o, o + SG_WIDTH))
    o += SG_WIDTH
    vn = _layer_norm_rows(_gelu(proj(o, o + SG_WIDTH)), ng_ref[...], nb_ref[...]).astype(BF16)
    o += SG_WIDTH
    ga_ref[...] = jax.nn.sigmoid(proj(o, o + D_MODEL))
    o += D_MODEL
    gate_b = jax.nn.sigmoid(proj(o, o + D_MODEL))

    chunks = []
    for c in range(INPROJ_TM // SG_CHUNK):
        vc = vn[c * SG_CHUNK:(c + 1) * SG_CHUNK, :]
        groups = [
            jnp.dot(ws_ref[g], vc[:, g * SG_GROUP_DIM:(g + 1) * SG_GROUP_DIM],
                    preferred_element_type=F32)
            for g in range(SG_GROUPS)
        ]
        chunks.append(jnp.concatenate(groups, axis=1) + bs_ref[...])
    y_sg = (u * jnp.concatenate(chunks, axis=0)).astype(BF16)
    sg_ref[...] = gate_b * jnp.dot(y_sg, wsg_ref[...], preferred_element_type=F32)


def _inproj(x, w_in_b, ng, nb, ws_b, bs_full, wsg_b):
    n_steps = SEQ // INPROJ_TM
    in_cols = w_in_b.shape[1]
    tok = lambda width: pl.BlockSpec((INPROJ_TM, width), lambda i: (i, 0))
    full = lambda shape: pl.BlockSpec(shape, lambda i: (0,) * len(shape))
    return pl.pallas_call(
        _inproj_kernel,
        grid=(n_steps,),
        in_specs=[
            tok(D_MODEL),
            full((D_MODEL, in_cols)),
            full((1, SG_WIDTH)),
            full((1, SG_WIDTH)),
            full((SG_GROUPS, SG_CHUNK, SG_CHUNK)),
            full((SG_CHUNK, SG_WIDTH)),
            full((SG_WIDTH, D_MODEL)),
        ],
        out_specs=[tok(NA_WIDTH), tok(NA_WIDTH), tok(NA_WIDTH), tok(D_MODEL), tok(D_MODEL)],
        out_shape=[
            jax.ShapeDtypeStruct((SEQ, NA_WIDTH), BF16),
            jax.ShapeDtypeStruct((SEQ, NA_WIDTH), BF16),
            jax.ShapeDtypeStruct((SEQ, NA_WIDTH), BF16),
            jax.ShapeDtypeStruct((SEQ, D_MODEL), F32),
            jax.ShapeDtypeStruct((SEQ, D_MODEL), F32),
        ],
        compiler_params=pltpu.CompilerParams(
            dimension_semantics=("arbitrary",), vmem_limit_bytes=VMEM_LIMIT),
        name="inproj",
    )(x, w_in_b, ng, nb, ws_b, bs_full, wsg_b)


def _natten_kernel(q_ref, kp_ref, kc_ref, kn_ref, vp_ref, vc_ref, vn_ref, bias_ref,
                   o_ref, kwin_ref, vwin_ref):
    i = pl.program_id(0)
    last = pl.num_programs(0) - 1
    kwin_ref[0:NA_ROWS] = kp_ref[...]
    kwin_ref[NA_ROWS:2 * NA_ROWS] = kc_ref[...]
    kwin_ref[2 * NA_ROWS:3 * NA_ROWS] = kn_ref[...]
    vwin_ref[0:NA_ROWS] = vp_ref[...]
    vwin_ref[NA_ROWS:2 * NA_ROWS] = vc_ref[...]
    vwin_ref[2 * NA_ROWS:3 * NA_ROWS] = vn_ref[...]

    n_keys = WIN_H * GRID_W
    lane_head = lax.broadcasted_iota(jnp.int32, (GRID_W, NA_WIDTH), 1) // NA_HEAD_DIM
    for j in range(NA_ROWS):
        off = jnp.where(i == 0, NA_ROWS, jnp.where(i == last, 0, j))
        dneg = jnp.where(i == 0, j, jnp.where(i == last, WIN_H // 2 + j, WIN_H // 2))
        kw = kwin_ref[pl.ds(off, WIN_H)].reshape(n_keys, NA_WIDTH)
        vw = vwin_ref[pl.ds(off, WIN_H)].reshape(n_keys, NA_WIDTH)
        qj = q_ref[j]
        qm = jnp.concatenate(
            [jnp.where(lane_head == h, qj, jnp.zeros_like(qj)) for h in range(NA_HEADS)], axis=0)
        logits = lax.dot_general(qm, kw, (((1,), (1,)), ((), ())), preferred_element_type=F32)
        logits = logits + bias_ref[dneg]
        m = jnp.max(logits, axis=-1, keepdims=True)
        p = jnp.exp(logits - m)
        inv = 1.0 / jnp.sum(p, axis=-1, keepdims=True)
        r = jnp.dot(p.astype(BF16), vw, preferred_element_type=F32) * inv
        acc = jnp.zeros((GRID_W, NA_WIDTH), F32)
        for h in range(NA_HEADS):
            acc = acc + jnp.where(lane_head == h, r[h * GRID_W:(h + 1) * GRID_W, :], 0.0)
        o_ref[j] = acc.astype(BF16)


def _natten(q, k, v, bias_tab):
    n_steps = GRID_ROWS // NA_ROWS
    q3 = q.reshape(GRID_ROWS, GRID_W, NA_WIDTH)
    k3 = k.reshape(GRID_ROWS, GRID_W, NA_WIDTH)
    v3 = v.reshape(GRID_ROWS, GRID_W, NA_WIDTH)
    blk = (NA_ROWS, GRID_W, NA_WIDTH)
    cur = pl.BlockSpec(blk, lambda i: (i, 0, 0))
    prev = pl.BlockSpec(blk, lambda i: (jnp.maximum(i - 1, 0), 0, 0))
    nxt = pl.BlockSpec(blk, lambda i: (jnp.minimum(i + 1, n_steps - 1), 0, 0))
    out = pl.pallas_call(
        _natten_kernel,
        grid=(n_steps,),
        in_specs=[cur, prev, cur, nxt, prev, cur, nxt,
                  pl.BlockSpec(bias_tab.shape, lambda i: (0, 0, 0))],
        out_specs=cur,
        out_shape=jax.ShapeDtypeStruct((GRID_ROWS, GRID_W, NA_WIDTH), BF16),
        scratch_shapes=[pltpu.VMEM((3 * NA_ROWS, GRID_W, NA_WIDTH), BF16),
                        pltpu.VMEM((3 * NA_ROWS, GRID_W, NA_WIDTH), BF16)],
        compiler_params=pltpu.CompilerParams(
            dimension_semantics=("arbitrary",), vmem_limit_bytes=VMEM_LIMIT),
        name="natten",
    )(q3, k3, k3, k3, v3, v3, v3, bias_tab)
    return out.reshape(SEQ, NA_WIDTH)


def _natten_bias_table(rpb):
    cols = np.arange(GRID_W)
    col_start = np.clip(cols - WIN_W // 2, 0, GRID_W - WIN_W)
    kc = np.arange(GRID_W)
    valid = (kc[None, :] >= col_start[:, None]) & (kc[None, :] < col_start[:, None] + WIN_W)
    dc = np.clip(kc[None, :] - cols[:, None] + (WIN_W - 1), 0, 2 * WIN_W - 2)
    dr = np.arange(WIN_H)[None, :] + (WIN_H - 1) - np.arange(WIN_H)[:, None]
    g = rpb[:, dr][:, :, :, dc]
    g = jnp.transpose(g, (1, 0, 3, 2, 4))
    g = jnp.where(jnp.asarray(valid)[None, None, :, None, :], g, MASK_BIAS)
    return g.reshape(WIN_H, NA_HEADS * GRID_W, WIN_H * GRID_W).astype(F32)


def _mix_kernel(x_ref, yna_ref, ga_ref, sg_ref, wna_ref, woutT_ref, g1_ref, b1_ref,
                wqT_ref, k1_ref, k2_ref,
                x1T_ref, x1Tb_ref, s1k_ref, s2k_ref, s2_ref):
    merged = ga_ref[...] * jnp.dot(yna_ref[...], wna_ref[...], preferred_element_type=F32)
    merged = (merged + sg_ref[...]).astype(BF16)
    outT = lax.dot_general(woutT_ref[...], merged, (((1,), (1,)), ((), ())),
                           preferred_element_type=F32)
    z = DN_ALPHA * x_ref[...].T + outT
    x1T = _layer_norm_cols(z, _tile_lanes(g1_ref[...], MIX_TM), _tile_lanes(b1_ref[...], MIX_TM))
    x1T_ref[...] = x1T
    x1Tb = x1T.astype(BF16)
    x1Tb_ref[...] = x1Tb
    qpT = jnp.dot(wqT_ref[...], x1Tb, preferred_element_type=F32).astype(BF16)
    for h in range(PEER_HEADS):
        lo = h * PEER_KEY_DIM
        s1 = jnp.dot(k1_ref[...], qpT[lo:lo + PEER_HALF, :], preferred_element_type=F32)
        s2 = jnp.dot(k2_ref[...], qpT[lo + PEER_HALF:lo + PEER_KEY_DIM, :],
                     preferred_element_type=F32)
        s2_ref[h] = s2
        for slab in range(MIX_TM // LANES):
            lanes = slice(slab * LANES, (slab + 1) * LANES)
            key_rows = pl.ds(h, PEER_N_KEYS, stride=PEER_HEADS)
            s1k_ref[slab, key_rows, :] = s1[:, lanes]
            s2k_ref[slab, key_rows, :] = s2[:, lanes]


def _mix(x, y_na, gate_a, sg_part, wna_b, woutT_b, g1, b1, wqT_b, k1_b, k2_b):
    n_steps = SEQ // MIX_TM
    tok = lambda width: pl.BlockSpec((MIX_TM, width), lambda i: (i, 0))
    full = lambda shape: pl.BlockSpec(shape, lambda i: (0,) * len(shape))
    featT = pl.BlockSpec((D_MODEL, MIX_TM), lambda i: (0, i))
    score = pl.BlockSpec((PEER_HEADS, PEER_N_KEYS, MIX_TM), lambda i: (0, 0, i))
    slabs = MIX_TM // LANES
    key_major = pl.BlockSpec((slabs, PEER_N_KEYS * PEER_HEADS, LANES), lambda i: (i, 0, 0))
    key_major_shape = jax.ShapeDtypeStruct((SEQ // LANES, PEER_N_KEYS * PEER_HEADS, LANES), F32)
    return pl.pallas_call(
        _mix_kernel,
        grid=(n_steps,),
        in_specs=[
            tok(D_MODEL), tok(NA_WIDTH), tok(D_MODEL), tok(D_MODEL),
            full((NA_WIDTH, D_MODEL)), full((D_MODEL, D_MODEL)),
            full((D_MODEL, LANES)), full((D_MODEL, LANES)),
            full((PEER_HEADS * PEER_KEY_DIM, D_MODEL)),
            full((PEER_N_KEYS, PEER_HALF)), full((PEER_N_KEYS, PEER_HALF)),
        ],
        out_specs=[featT, featT, key_major, key_major, score],
        out_shape=[
            jax.ShapeDtypeStruct((D_MODEL, SEQ), F32),
            jax.ShapeDtypeStruct((D_MODEL, SEQ), BF16),
            key_major_shape,
            key_major_shape,
            jax.ShapeDtypeStruct((PEER_HEADS, PEER_N_KEYS, SEQ), F32),
        ],
        compiler_params=pltpu.CompilerParams(
            dimension_semantics=("arbitrary",), vmem_limit_bytes=VMEM_LIMIT),
        name="mix",
    )(x, y_na, gate_a, sg_part, wna_b, woutT_b, g1, b1, wqT_b, k1_b, k2_b)


def _cmpx(xs, i, j):
    hi = jnp.maximum(xs[i], xs[j])
    lo = jnp.minimum(xs[i], xs[j])
    xs[i], xs[j] = hi, lo


def _bitonic_sort_desc(xs):
    xs = list(xs)
    n = len(xs)
    size = 2
    while size <= n:
        stride = size // 2
        while stride >= 1:
            for i in range(n):
                l = i ^ stride
                if l > i:
                    if (i & size) == 0:
                        _cmpx(xs, i, l)
                    else:
                        _cmpx(xs, l, i)
            stride //= 2
        size *= 2
    return xs


def _bitonic_merge_desc(xs):
    xs = list(xs)
    n = len(xs)
    stride = n // 2
    while stride >= 1:
        for i in range(n):
            if (i & stride) == 0:
                _cmpx(xs, i, i + stride)
        stride //= 2
    return xs


def _top_merge(xs, ys):
    n = len(xs)
    return _bitonic_merge_desc([jnp.maximum(xs[i], ys[n - 1 - i]) for i in range(n)])


def _top16_of(vals):
    k = PEER_TOPK
    pad = (-len(vals)) % k
    vals = list(vals) + [jnp.full_like(vals[0], -jnp.inf)] * pad
    groups = [_bitonic_sort_desc(vals[g:g + k]) for g in range(0, len(vals), k)]
    top = groups[0]
    for grp in groups[1:]:
        top = _top_merge(top, grp)
    return top


def _route_kernel(s1k_ref, s2k_ref, cnt_ref, c1_ref, v2_ref):
    def key(ref, a):
        return ref[0, SUBLANES * a:SUBLANES * (a + 1), :]

    v1 = _top16_of([key(s1k_ref, a) for a in range(PEER_N_KEYS)])
    v2 = _top16_of([key(s2k_ref, b) for b in range(PEER_N_KEYS)])
    cands = [v1[i] + v2[j] for i in range(PEER_TOPK) for j in range(PEER_TOPK)
             if (i + 1) * (j + 1) <= PEER_TOPK]
    top = _top16_of(cands)
    tau = top[PEER_TOPK - 1]
    z = jnp.ones_like(tau)
    for t in top[1:]:
        z = z + jnp.exp(t - top[0])
    inv_z = 1.0 / z
    for j in range(PEER_TOPK):
        v2_ref[SUBLANES * j:SUBLANES * (j + 1), :] = v2[j]
    for a in range(PEER_N_KEYS):
        s1a = key(s1k_ref, a)
        cnt = jnp.zeros((SUBLANES, LANES), F32)
        for j in range(PEER_TOPK):
            cnt = jnp.where(s1a + v2[j] >= tau, float(j + 1), cnt)
        cnt_ref[0, SUBLANES * a:SUBLANES * (a + 1), :] = cnt
        c1_ref[0, SUBLANES * a:SUBLANES * (a + 1), :] = jnp.exp(s1a - v1[0]) * inv_z


def _route(s1k, s2k):
    n_steps = SEQ // LANES
    blk = pl.BlockSpec((1, PEER_N_KEYS * PEER_HEADS, LANES), lambda i: (i, 0, 0))
    shp = jax.ShapeDtypeStruct((n_steps, PEER_N_KEYS * PEER_HEADS, LANES), F32)
    top_rows = PEER_TOPK * PEER_HEADS
    return pl.pallas_call(
        _route_kernel,
        grid=(n_steps,),
        in_specs=[blk, blk],
        out_specs=[blk, blk, pl.BlockSpec((top_rows, LANES), lambda i: (0, i))],
        out_shape=[shp, shp, jax.ShapeDtypeStruct((top_rows, SEQ), F32)],
        compiler_params=pltpu.CompilerParams(
            dimension_semantics=("arbitrary",), vmem_limit_bytes=VMEM_LIMIT),
        name="route",
    )(s1k, s2k)


def _peer_kernel(x1T_ref, x1Tb_ref, u_ref, vT_ref, s2_ref, v2_ref, cnt_ref, c1_ref, g2_ref, b2_ref,
                 o_ref, acc_ref, p2_ref, r2_ref, a0_ref, a1_ref, w0_ref, w1_ref):
    e = pl.program_id(1)

    @pl.when(e == 0)
    def _():
        acc_ref[...] = jnp.zeros_like(acc_ref)
        for h in range(PEER_HEADS):
            s2h = s2_ref[h]
            rank = jnp.zeros_like(s2h)
            for j in range(PEER_TOPK):
                row = j * PEER_HEADS + h
                rank = jnp.where(v2_ref[row:row + 1, :] > s2h, float(j + 1), rank)
            r2_ref[h] = rank.astype(BF16)
            p2_ref[h] = jnp.exp(s2h - v2_ref[h:h + 1, :]).astype(BF16)

    blocks_per_tile = PEER_ET // PEER_N_KEYS
    n_groups = PEER_ET // PEER_GROUP
    blocks_per_group = PEER_GROUP // PEER_N_KEYS
    a_bufs = (a0_ref, a1_ref)
    w_bufs = (w0_ref, w1_ref)

    def scores(grp):
        rows = slice(grp * PEER_GROUP, (grp + 1) * PEER_GROUP)
        return jnp.dot(u_ref[rows, :], x1Tb_ref[...], preferred_element_type=F32)

    def contract(grp):
        cols = slice(grp * PEER_GROUP, (grp + 1) * PEER_GROUP)
        acc_ref[...] += jnp.dot(vT_ref[:, cols], w_bufs[grp % 2][...], preferred_element_type=F32)

    def gate(grp):
        a_ref, w_ref = a_bufs[grp % 2], w_bufs[grp % 2]
        for blk in range(blocks_per_group):
            a = e * blocks_per_tile + grp * blocks_per_group + blk
            head_rows = pl.ds(pl.multiple_of(a * PEER_HEADS, PEER_HEADS), PEER_HEADS)
            for lt in range(PEER_TT // LANES):
                lanes = slice(lt * LANES, (lt + 1) * LANES)
                cnt_heads = cnt_ref[lt, head_rows, :].astype(BF16)
                c1_heads = c1_ref[lt, head_rows, :].astype(BF16)
                for rc in range(PEER_N_KEYS // PEER_CHUNK_ROWS):
                    rows = slice(rc * PEER_CHUNK_ROWS, (rc + 1) * PEER_CHUNK_ROWS)
                    g = None
                    for h in range(PEER_HEADS):
                        cnt = cnt_heads[h:h + 1, :]
                        c1 = c1_heads[h:h + 1, :]
                        term = jnp.where(r2_ref[h, rows, lanes] < cnt,
                                         p2_ref[h, rows, lanes] * c1, jnp.zeros((), BF16))
                        g = term if g is None else g + term
                    out_rows = slice(blk * PEER_N_KEYS + rc * PEER_CHUNK_ROWS,
                                     blk * PEER_N_KEYS + (rc + 1) * PEER_CHUNK_ROWS)
                    w_ref[out_rows, lanes] = _gelu_times(a_ref[out_rows, lanes], g)

    a_bufs[0][...] = scores(0)
    for grp in range(n_groups):
        if grp + 1 < n_groups:
            a_bufs[(grp + 1) % 2][...] = scores(grp + 1)
        gate(grp)
        if grp > 0:
            contract(grp - 1)
    contract(n_groups - 1)

    @pl.when(e == pl.num_programs(1) - 1)
    def _():
        z = DN_ALPHA * x1T_ref[...] + acc_ref[...]
        y = _layer_norm_cols(z, _tile_lanes(g2_ref[...], PEER_TT), _tile_lanes(b2_ref[...], PEER_TT))
        o_ref[...] = y.T


def _peer(x1T, x1Tb, u_b, vT_b, s2T, v2, cnt, c1, g2, b2):
    n_tok = SEQ // PEER_TT
    n_exp = PEER_N_EXPERTS // PEER_ET
    featT = pl.BlockSpec((D_MODEL, PEER_TT), lambda j, e: (0, j))
    side = pl.BlockSpec((PEER_HEADS, PEER_N_KEYS, PEER_TT), lambda j, e: (0, 0, j))
    key_major = pl.BlockSpec((PEER_TT // LANES, PEER_N_KEYS * PEER_HEADS, LANES),
                             lambda j, e: (j, 0, 0))
    par = pl.BlockSpec((D_MODEL, LANES), lambda j, e: (0, 0))
    return pl.pallas_call(
        _peer_kernel,
        grid=(n_tok, n_exp),
        in_specs=[
            featT, featT,
            pl.BlockSpec((PEER_ET, D_MODEL), lambda j, e: (e, 0)),
            pl.BlockSpec((D_MODEL, PEER_ET), lambda j, e: (0, e)),
            side,
            pl.BlockSpec((PEER_TOPK * PEER_HEADS, PEER_TT), lambda j, e: (0, j)),
            key_major, key_major, par, par,
        ],
        out_specs=pl.BlockSpec((PEER_TT, D_MODEL), lambda j, e: (j, 0)),
        out_shape=jax.ShapeDtypeStruct((SEQ, D_MODEL), F32),
        scratch_shapes=[
            pltpu.VMEM((D_MODEL, PEER_TT), F32),
            pltpu.VMEM((PEER_HEADS, PEER_N_KEYS, PEER_TT), BF16),
            pltpu.VMEM((PEER_HEADS, PEER_N_KEYS, PEER_TT), BF16),
            pltpu.VMEM((PEER_GROUP, PEER_TT), F32),
            pltpu.VMEM((PEER_GROUP, PEER_TT), F32),
            pltpu.VMEM((PEER_GROUP, PEER_TT), BF16),
            pltpu.VMEM((PEER_GROUP, PEER_TT), BF16),
        ],
        compiler_params=pltpu.CompilerParams(
            dimension_semantics=("arbitrary", "arbitrary"), vmem_limit_bytes=VMEM_LIMIT),
        name="peer",
    )(x1T, x1Tb, u_b, vT_b, s2T, v2, cnt, c1, g2, b2)


def _lane_replicated(p):
    return jnp.broadcast_to(p.astype(F32)[:, None], (p.shape[0], LANES))


def kernel(x, w_in, na_rpb, sg_norm_g, sg_norm_b, sg_spatial_w, sg_spatial_b, w_branch_na,
           w_branch_sg, w_out, ln1_g, ln1_b, peer_wq, peer_subkeys1, peer_subkeys2, peer_u, peer_v,
           ln2_g, ln2_b):
    assert x.shape == (1, SEQ, D_MODEL) and w_in.shape[0] == 1
    l = 0
    xs = x[0]
    q, k, v, gate_a, sg_part = _inproj(
        xs,
        w_in[l].astype(BF16),
        sg_norm_g[l][None, :], sg_norm_b[l][None, :],
        sg_spatial_w[l].astype(BF16),
        jnp.repeat(sg_spatial_b[l].T, SG_GROUP_DIM, axis=1),
        w_branch_sg[l].astype(BF16),
    )
    y_na = _natten(q, k, v, _natten_bias_table(na_rpb[l]))
    x1T, x1Tb, s1k, s2k, s2T = _mix(
        xs, y_na, gate_a, sg_part,
        w_branch_na[l].astype(BF16),
        w_out[l].T.astype(BF16),
        _lane_replicated(ln1_g[l]), _lane_replicated(ln1_b[l]),
        peer_wq[l].T.astype(BF16),
        peer_subkeys1[l].astype(BF16), peer_subkeys2[l].astype(BF16),
    )
    cnt, c1, v2 = _route(s1k, s2k)
    out = _peer(
        x1T, x1Tb,
        peer_u[l].astype(BF16), peer_v[l].T.astype(BF16),
        s2T, v2, cnt, c1,
        _lane_replicated(ln2_g[l]), _lane_replicated(ln2_b[l]),
    )
    return out[None]
```

```python
import functools
import math

import numpy as np
import jax
import jax.numpy as jnp
from jax import lax
from jax.experimental import pallas as pl
from jax.experimental.pallas import tpu as pltpu

F32 = jnp.float32
BF16 = jnp.bfloat16

D_MODEL = 1024
SEQ = 16384
GRID_W = 64
GRID_ROWS = SEQ // GRID_W
WIN_H = 8
WIN_W = 16
NA_HEADS = 8
NA_HEAD_DIM = 64
NA_WIDTH = NA_HEADS * NA_HEAD_DIM
SG_GROUPS = 4
SG_GROUP_DIM = 128
SG_WIDTH = SG_GROUPS * SG_GROUP_DIM
SG_CHUNK = 128
PEER_HEADS = 8
PEER_N_KEYS = 128
PEER_N_EXPERTS = PEER_N_KEYS * PEER_N_KEYS
PEER_KEY_DIM = 256
PEER_HALF = PEER_KEY_DIM // 2
PEER_TOPK = 16
DN_ALPHA = 2.0 ** 0.25
LN_EPS = 1e-5
MASK_BIAS = -1e30
GELU_C0 = math.sqrt(2.0 / math.pi)
GELU_C1 = 0.044715 * GELU_C0

LANES = 128
SUBLANES = 8

INPROJ_TM = 512
NA_ROWS = 4
MIX_TM = 512
ROUTE_TT = 512
XPOSE_ROWS = 512
PEER_TT = 512
PEER_ET = 2048
PEER_GROUP = 512
PEER_CHUNK_ROWS = 128
VMEM_LIMIT = 56 * 1024 * 1024


def _gelu(x):
    return jax.nn.gelu(x)


def _gelu_times(x, g):
    u = x * (GELU_C0 + GELU_C1 * (x * x))
    t = jnp.tanh(u.astype(BF16))
    return (0.5 * x.astype(BF16)) * (1.0 + t) * g


def _layer_norm_rows(x, g, b):
    mu = jnp.mean(x, axis=-1, keepdims=True)
    xc = x - mu
    var = jnp.mean(xc * xc, axis=-1, keepdims=True)
    return xc * lax.rsqrt(var + LN_EPS) * g + b


def _layer_norm_cols(z, g, b):
    mu = jnp.mean(z, axis=0, keepdims=True)
    zc = z - mu
    var = jnp.mean(zc * zc, axis=0, keepdims=True)
    return zc * lax.rsqrt(var + LN_EPS) * g + b


def _tile_lanes(p128, width):
    reps = width // LANES
    return p128 if reps == 1 else jnp.concatenate([p128] * reps, axis=1)


def _inproj_kernel(x_ref, w_ref, ng_ref, nb_ref, ws_ref, bs_ref, wsg_ref,
                   q_ref, k_ref, v_ref, ga_ref, sg_ref):
    xb = x_ref[...].astype(BF16)

    def proj(lo, hi):
        return jnp.dot(xb, w_ref[:, lo:hi], preferred_element_type=F32)

    o = 0
    q_ref[...] = (proj(o, o + NA_WIDTH) * (NA_HEAD_DIM ** -0.5)).astype(BF16)
    o += NA_WIDTH
    k_ref[...] = proj(o, o + NA_WIDTH).astype(BF16)
    o += NA_WIDTH
    v_ref[...] = proj(o, o + NA_WIDTH).astype(BF16)
    o += NA_WIDTH
    u = _gelu(proj(o, o + SG_WIDTH))
    o += SG_WIDTH
    vn = _layer_norm_rows(_gelu(proj(o, o + SG_WIDTH)), ng_ref[...], nb_ref[...]).astype(BF16)
    o += SG_WIDTH
    ga_ref[...] = jax.nn.sigmoid(proj(o, o + D_MODEL))
    o += D_MODEL
    gate_b = jax.nn.sigmoid(proj(o, o + D_MODEL))

    chunks = []
    for c in range(INPROJ_TM // SG_CHUNK):
        vc = vn[c * SG_CHUNK:(c + 1) * SG_CHUNK, :]
        groups = [
            jnp.dot(ws_ref[g], vc[:, g * SG_GROUP_DIM:(g + 1) * SG_GROUP_DIM],
                    preferred_element_type=F32)
            for g in range(SG_GROUPS)
        ]
        chunks.append(jnp.concatenate(groups, axis=1) + bs_ref[...])
    y_sg = (u * jnp.concatenate(chunks, axis=0)).astype(BF16)
    sg_ref[...] = gate_b * jnp.dot(y_sg, wsg_ref[...], preferred_element_type=F32)


def _inproj(x, w_in_b, ng, nb, ws_b, bs_full, wsg_b):
    n_steps = SEQ // INPROJ_TM
    in_cols = w_in_b.shape[1]
    tok = lambda width: pl.BlockSpec((INPROJ_TM, width), lambda i: (i, 0))
    full = lambda shape: pl.BlockSpec(shape, lambda i: (0,) * len(shape))
    return pl.pallas_call(
        _inproj_kernel,
        grid=(n_steps,),
        in_specs=[
            tok(D_MODEL),
            full((D_MODEL, in_cols)),
            full((1, SG_WIDTH)),
            full((1, SG_WIDTH)),
            full((SG_GROUPS, SG_CHUNK, SG_CHUNK)),
            full((SG_CHUNK, SG_WIDTH)),
            full((SG_WIDTH, D_MODEL)),
        ],
        out_specs=[tok(NA_WIDTH), tok(NA_WIDTH), tok(NA_WIDTH), tok(D_MODEL), tok(D_MODEL)],
        out_shape=[
            jax.ShapeDtypeStruct((SEQ, NA_WIDTH), BF16),
            jax.ShapeDtypeStruct((SEQ, NA_WIDTH), BF16),
            jax.ShapeDtypeStruct((SEQ, NA_WIDTH), BF16),
            jax.ShapeDtypeStruct((SEQ, D_MODEL), F32),
            jax.ShapeDtypeStruct((SEQ, D_MODEL), F32),
        ],
        compiler_params=pltpu.CompilerParams(
            dimension_semantics=("arbitrary",), vmem_limit_bytes=VMEM_LIMIT),
        name="inproj",
    )(x, w_in_b, ng, nb, ws_b, bs_full, wsg_b)


def _natten_kernel(q_ref, kp_ref, kc_ref, kn_ref, vp_ref, vc_ref, vn_ref, bias_ref,
                   o_ref, kwin_ref, vwin_ref):
    i = pl.program_id(0)
    last = pl.num_programs(0) - 1
    kwin_ref[0:NA_ROWS] = kp_ref[...]
    kwin_ref[NA_ROWS:2 * NA_ROWS] = kc_ref[...]
    kwin_ref[2 * NA_ROWS:3 * NA_ROWS] = kn_ref[...]
    vwin_ref[0:NA_ROWS] = vp_ref[...]
    vwin_ref[NA_ROWS:2 * NA_ROWS] = vc_ref[...]
    vwin_ref[2 * NA_ROWS:3 * NA_ROWS] = vn_ref[...]

    n_keys = WIN_H * GRID_W
    lane_head = lax.broadcasted_iota(jnp.int32, (GRID_W, NA_WIDTH), 1) // NA_HEAD_DIM
    for j in range(NA_ROWS):
        off = jnp.where(i == 0, NA_ROWS, jnp.where(i == last, 0, j))
        dneg = jnp.where(i == 0, j, jnp.where(i == last, WIN_H // 2 + j, WIN_H // 2))
        kw = kwin_ref[pl.ds(off, WIN_H)].reshape(n_keys, NA_WIDTH)
        vw = vwin_ref[pl.ds(off, WIN_H)].reshape(n_keys, NA_WIDTH)
        qj = q_ref[j]
        qm = jnp.concatenate(
            [jnp.where(lane_head == h, qj, jnp.zeros_like(qj)) for h in range(NA_HEADS)], axis=0)
        logits = lax.dot_general(qm, kw, (((1,), (1,)), ((), ())), preferred_element_type=F32)
        logits = logits + bias_ref[dneg]
        m = jnp.max(logits, axis=-1, keepdims=True)
        p = jnp.exp(logits - m)
        inv = 1.0 / jnp.sum(p, axis=-1, keepdims=True)
        r = jnp.dot(p.astype(BF16), vw, preferred_element_type=F32) * inv
        acc = jnp.zeros((GRID_W, NA_WIDTH), F32)
        for h in range(NA_HEADS):
            acc = acc + jnp.where(lane_head == h, r[h * GRID_W:(h + 1) * GRID_W, :], 0.0)
        o_ref[j] = acc.astype(BF16)


def _natten(q, k, v, bias_tab):
    n_steps = GRID_ROWS // NA_ROWS
    q3 = q.reshape(GRID_ROWS, GRID_W, NA_WIDTH)
    k3 = k.reshape(GRID_ROWS, GRID_W, NA_WIDTH)
    v3 = v.reshape(GRID_ROWS, GRID_W, NA_WIDTH)
    blk = (NA_ROWS, GRID_W, NA_WIDTH)
    cur = pl.BlockSpec(blk, lambda i: (i, 0, 0))
    prev = pl.BlockSpec(blk, lambda i: (jnp.maximum(i - 1, 0), 0, 0))
    nxt = pl.BlockSpec(blk, lambda i: (jnp.minimum(i + 1, n_steps - 1), 0, 0))
    out = pl.pallas_call(
        _natten_kernel,
        grid=(n_steps,),
        in_specs=[cur, prev, cur, nxt, prev, cur, nxt,
                  pl.BlockSpec(bias_tab.shape, lambda i: (0, 0, 0))],
        out_specs=cur,
        out_shape=jax.ShapeDtypeStruct((GRID_ROWS, GRID_W, NA_WIDTH), BF16),
        scratch_shapes=[pltpu.VMEM((3 * NA_ROWS, GRID_W, NA_WIDTH), BF16),
                        pltpu.VMEM((3 * NA_ROWS, GRID_W, NA_WIDTH), BF16)],
        compiler_params=pltpu.CompilerParams(
            dimension_semantics=("arbitrary",), vmem_limit_bytes=VMEM_LIMIT),
        name="natten",
    )(q3, k3, k3, k3, v3, v3, v3, bias_tab)
    return out.reshape(SEQ, NA_WIDTH)


def _natten_bias_table(rpb):
    cols = np.arange(GRID_W)
    col_start = np.clip(cols - WIN_W // 2, 0, GRID_W - WIN_W)
    kc = np.arange(GRID_W)
    valid = (kc[None, :] >= col_start[:, None]) & (kc[None, :] < col_start[:, None] + WIN_W)
    dc = np.where(valid, kc[None, :] - cols[:, None] + (WIN_W - 1), -1)
    dr = np.arange(WIN_H)[None, :] + (WIN_H - 1) - np.arange(WIN_H)[:, None]
    rows = jnp.transpose(rpb[:, dr], (1, 0, 2, 3)).astype(F32)
    dc5 = jnp.asarray(dc)[None, None, :, None, :]
    shape5 = (WIN_H, NA_HEADS, GRID_W, WIN_H, GRID_W)
    table = jnp.full(shape5, MASK_BIAS, F32)
    for d in range(2 * WIN_W - 1):
        table = jnp.where(dc5 == d, rows[:, :, None, :, d, None], table)
    return table.reshape(WIN_H, NA_HEADS * GRID_W, WIN_H * GRID_W)


def _mix_kernel(x_ref, yna_ref, ga_ref, sg_ref, wna_ref, woutT_ref, g1_ref, b1_ref,
                x1T_ref, x1Tb_ref):
    merged = ga_ref[...] * jnp.dot(yna_ref[...], wna_ref[...], preferred_element_type=F32)
    merged = (merged + sg_ref[...]).astype(BF16)
    outT = lax.dot_general(woutT_ref[...], merged, (((1,), (1,)), ((), ())),
                           preferred_element_type=F32)
    z = DN_ALPHA * x_ref[...].T + outT
    x1T = _layer_norm_cols(z, _tile_lanes(g1_ref[...], MIX_TM), _tile_lanes(b1_ref[...], MIX_TM))
    x1T_ref[...] = x1T
    x1Tb_ref[...] = x1T.astype(BF16)


def _mix(x, y_na, gate_a, sg_part, wna_b, woutT_b, g1, b1):
    n_steps = SEQ // MIX_TM
    tok = lambda width: pl.BlockSpec((MIX_TM, width), lambda i: (i, 0))
    full = lambda shape: pl.BlockSpec(shape, lambda i: (0,) * len(shape))
    featT = pl.BlockSpec((D_MODEL, MIX_TM), lambda i: (0, i))
    return pl.pallas_call(
        _mix_kernel,
        grid=(n_steps,),
        in_specs=[
            tok(D_MODEL), tok(NA_WIDTH), tok(D_MODEL), tok(D_MODEL),
            full((NA_WIDTH, D_MODEL)), full((D_MODEL, D_MODEL)),
            full((D_MODEL, LANES)), full((D_MODEL, LANES)),
        ],
        out_specs=[featT, featT],
        out_shape=[
            jax.ShapeDtypeStruct((D_MODEL, SEQ), F32),
            jax.ShapeDtypeStruct((D_MODEL, SEQ), BF16),
        ],
        compiler_params=pltpu.CompilerParams(
            dimension_semantics=("arbitrary",), vmem_limit_bytes=VMEM_LIMIT),
        name="mix",
    )(x, y_na, gate_a, sg_part, wna_b, woutT_b, g1, b1)


def _cmpx(xs, i, j):
    hi = jnp.maximum(xs[i], xs[j])
    lo = jnp.minimum(xs[i], xs[j])
    xs[i], xs[j] = hi, lo


def _bitonic_sort_desc(xs):
    xs = list(xs)
    n = len(xs)
    size = 2
    while size <= n:
        stride = size // 2
        while stride >= 1:
            for i in range(n):
                l = i ^ stride
                if l > i:
                    if (i & size) == 0:
                        _cmpx(xs, i, l)
                    else:
                        _cmpx(xs, l, i)
            stride //= 2
        size *= 2
    return xs


def _bitonic_merge_desc(xs):
    xs = list(xs)
    n = len(xs)
    stride = n // 2
    while stride >= 1:
        for i in range(n):
            if (i & stride) == 0:
                _cmpx(xs, i, i + stride)
        stride //= 2
    return xs


def _top_merge(xs, ys):
    n = len(xs)
    return _bitonic_merge_desc([jnp.maximum(xs[i], ys[n - 1 - i]) for i in range(n)])


def _top16_of(vals):
    k = PEER_TOPK
    pad = (-len(vals)) % k
    vals = list(vals) + [jnp.full_like(vals[0], -jnp.inf)] * pad
    groups = [_bitonic_sort_desc(vals[g:g + k]) for g in range(0, len(vals), k)]
    top = groups[0]
    for grp in groups[1:]:
        top = _top_merge(top, grp)
    return top


def _route_kernel(x1Tb_ref, wqT_ref, k1_ref, k2_ref,
                  cnt_ref, c1_ref, v2_ref, s2_ref, s1k_ref, s2k_ref):
    qpT = jnp.dot(wqT_ref[...], x1Tb_ref[...], preferred_element_type=F32).astype(BF16)
    for h in range(PEER_HEADS):
        lo = h * PEER_KEY_DIM
        s1 = jnp.dot(k1_ref[...], qpT[lo:lo + PEER_HALF, :], preferred_element_type=F32)
        s2 = jnp.dot(k2_ref[...], qpT[lo + PEER_HALF:lo + PEER_KEY_DIM, :],
                     preferred_element_type=F32)
        s2_ref[h] = s2
        for slab in range(ROUTE_TT // LANES):
            lanes = slice(slab * LANES, (slab + 1) * LANES)
            key_rows = pl.ds(h, PEER_N_KEYS, stride=PEER_HEADS)
            s1k_ref[slab, key_rows, :] = s1[:, lanes]
            s2k_ref[slab, key_rows, :] = s2[:, lanes]

    def route_slab(slab, carry):
        def key(ref, a):
            return ref[slab, SUBLANES * a:SUBLANES * (a + 1), :]

        v1 = _top16_of([key(s1k_ref, a) for a in range(PEER_N_KEYS)])
        v2 = _top16_of([key(s2k_ref, b) for b in range(PEER_N_KEYS)])
        cands = [v1[i] + v2[j] for i in range(PEER_TOPK) for j in range(PEER_TOPK)
                 if (i + 1) * (j + 1) <= PEER_TOPK]
        top = _top16_of(cands)
        tau = top[PEER_TOPK - 1]
        z = jnp.ones_like(tau)
        for t in top[1:]:
            z = z + jnp.exp(t - top[0])
        inv_z = 1.0 / z
        need = []
        for j in range(PEER_TOPK):
            v2_ref[slab, SUBLANES * j:SUBLANES * (j + 1), :] = v2[j]
            nj = jnp.full((SUBLANES, LANES), jnp.inf, F32)
            for i in range(PEER_TOPK):
                nj = jnp.where(v1[i] + v2[j] >= tau, v1[i], nj)
            need.append(nj)
        for a in range(PEER_N_KEYS):
            s1a = key(s1k_ref, a)
            cnt = jnp.zeros((SUBLANES, LANES), F32)
            for j in range(PEER_TOPK):
                cnt = jnp.where(s1a >= need[j], float(j + 1), cnt)
            cnt_ref[slab, SUBLANES * a:SUBLANES * (a + 1), :] = cnt
            c1_ref[slab, SUBLANES * a:SUBLANES * (a + 1), :] = jnp.exp(s1a - v1[0]) * inv_z
        return carry

    lax.fori_loop(0, ROUTE_TT // LANES, route_slab, 0)


def _route(x1Tb, wqT_b, k1_b, k2_b):
    n_steps = SEQ // ROUTE_TT
    slabs = ROUTE_TT // LANES
    key_rows = PEER_N_KEYS * PEER_HEADS
    top_rows = PEER_TOPK * PEER_HEADS
    full = lambda shape: pl.BlockSpec(shape, lambda i: (0,) * len(shape))
    key_major = pl.BlockSpec((slabs, key_rows, LANES), lambda i: (i, 0, 0))
    key_major_shape = jax.ShapeDtypeStruct((SEQ // LANES, key_rows, LANES), F32)
    return pl.pallas_call(
        _route_kernel,
        grid=(n_steps,),
        in_specs=[
            pl.BlockSpec((D_MODEL, ROUTE_TT), lambda i: (0, i)),
            full((PEER_HEADS * PEER_KEY_DIM, D_MODEL)),
            full((PEER_N_KEYS, PEER_HALF)), full((PEER_N_KEYS, PEER_HALF)),
        ],
        out_specs=[
            key_major, key_major,
            pl.BlockSpec((slabs, top_rows, LANES), lambda i: (i, 0, 0)),
            pl.BlockSpec((PEER_HEADS, PEER_N_KEYS, ROUTE_TT), lambda i: (0, 0, i)),
        ],
        out_shape=[
            key_major_shape, key_major_shape,
            jax.ShapeDtypeStruct((SEQ // LANES, top_rows, LANES), F32),
            jax.ShapeDtypeStruct((PEER_HEADS, PEER_N_KEYS, SEQ), F32),
        ],
        scratch_shapes=[pltpu.VMEM((slabs, key_rows, LANES), F32),
                        pltpu.VMEM((slabs, key_rows, LANES), F32)],
        compiler_params=pltpu.CompilerParams(
            dimension_semantics=("arbitrary",), vmem_limit_bytes=VMEM_LIMIT),
        name="route",
    )(x1Tb, wqT_b, k1_b, k2_b)


def _peer_kernel(x1T_ref, x1Tb_ref, u_ref, vT_ref, s2_ref, v2_ref, cnt_ref, c1_ref, g2_ref, b2_ref,
                 o_ref, acc_ref, p2_ref, r2_ref, a0_ref, a1_ref, w0_ref, w1_ref):
    e = pl.program_id(1)

    @pl.when(e == 0)
    def _():
        acc_ref[...] = jnp.zeros_like(acc_ref)
        for h in range(PEER_HEADS):
            for lt in range(PEER_TT // LANES):
                lanes = slice(lt * LANES, (lt + 1) * LANES)
                s2h = s2_ref[h, :, lanes]
                rank = jnp.zeros_like(s2h)
                for j in range(PEER_TOPK):
                    row = j * PEER_HEADS + h
                    rank = jnp.where(v2_ref[lt, row:row + 1, :] > s2h, float(j + 1), rank)
                r2_ref[h, :, lanes] = rank.astype(BF16)
                p2_ref[h, :, lanes] = jnp.exp(s2h - v2_ref[lt, h:h + 1, :]).astype(BF16)

    blocks_per_tile = PEER_ET // PEER_N_KEYS
    n_groups = PEER_ET // PEER_GROUP
    blocks_per_group = PEER_GROUP // PEER_N_KEYS
    a_bufs = (a0_ref, a1_ref)
    w_bufs = (w0_ref, w1_ref)

    def scores(grp):
        rows = slice(grp * PEER_GROUP, (grp + 1) * PEER_GROUP)
        return jnp.dot(u_ref[rows, :], x1Tb_ref[...], preferred_element_type=F32)

    def contract(grp):
        cols = slice(grp * PEER_GROUP, (grp + 1) * PEER_GROUP)
        acc_ref[...] += jnp.dot(vT_ref[:, cols], w_bufs[grp % 2][...], preferred_element_type=F32)

    def gate(grp):
        a_ref, w_ref = a_bufs[grp % 2], w_bufs[grp % 2]
        for blk in range(blocks_per_group):
            a = e * blocks_per_tile + grp * blocks_per_group + blk
            head_rows = pl.ds(pl.multiple_of(a * PEER_HEADS, PEER_HEADS), PEER_HEADS)
            for lt in range(PEER_TT // LANES):
                lanes = slice(lt * LANES, (lt + 1) * LANES)
                cnt_heads = cnt_ref[lt, head_rows, :].astype(BF16)
                c1_heads = c1_ref[lt, head_rows, :].astype(BF16)
                for rc in range(PEER_N_KEYS // PEER_CHUNK_ROWS):
                    rows = slice(rc * PEER_CHUNK_ROWS, (rc + 1) * PEER_CHUNK_ROWS)
                    g = None
                    for h in range(PEER_HEADS):
                        cnt = cnt_heads[h:h + 1, :]
                        c1 = c1_heads[h:h + 1, :]
                        term = jnp.where(r2_ref[h, rows, lanes] < cnt,
                                         p2_ref[h, rows, lanes] * c1, jnp.zeros((), BF16))
                        g = term if g is None else g + term
                    out_rows = slice(blk * PEER_N_KEYS + rc * PEER_CHUNK_ROWS,
                                     blk * PEER_N_KEYS + (rc + 1) * PEER_CHUNK_ROWS)
                    w_ref[out_rows, lanes] = _gelu_times(a_ref[out_rows, lanes], g)

    a_bufs[0][...] = scores(0)
    for grp in range(n_groups):
        if grp + 1 < n_groups:
            a_bufs[(grp + 1) % 2][...] = scores(grp + 1)
        gate(grp)
        if grp > 0:
            contract(grp - 1)
    contract(n_groups - 1)

    @pl.when(e == pl.num_programs(1) - 1)
    def _():
        z = DN_ALPHA * x1T_ref[...] + acc_ref[...]
        y = _layer_norm_cols(z, _tile_lanes(g2_ref[...], PEER_TT), _tile_lanes(b2_ref[...], PEER_TT))
        o_ref[...] = y.T


def _peer(x1T, x1Tb, u_b, vT_b, s2T, v2, cnt, c1, g2, b2):
    n_tok = SEQ // PEER_TT
    n_exp = PEER_N_EXPERTS // PEER_ET
    featT = pl.BlockSpec((D_MODEL, PEER_TT), lambda j, e: (0, j))
    side = pl.BlockSpec((PEER_HEADS, PEER_N_KEYS, PEER_TT), lambda j, e: (0, 0, j))
    key_major = pl.BlockSpec((PEER_TT // LANES, PEER_N_KEYS * PEER_HEADS, LANES),
                             lambda j, e: (j, 0, 0))
    par = pl.BlockSpec((D_MODEL, LANES), lambda j, e: (0, 0))
    return pl.pallas_call(
        _peer_kernel,
        grid=(n_tok, n_exp),
        in_specs=[
            featT, featT,
            pl.BlockSpec((PEER_ET, D_MODEL), lambda j, e: (e, 0)),
            pl.BlockSpec((D_MODEL, PEER_ET), lambda j, e: (0, e)),
            side,
            pl.BlockSpec((PEER_TT // LANES, PEER_TOPK * PEER_HEADS, LANES), lambda j, e: (j, 0, 0)),
            key_major, key_major, par, par,
        ],
        out_specs=pl.BlockSpec((PEER_TT, D_MODEL), lambda j, e: (j, 0)),
        out_shape=jax.ShapeDtypeStruct((SEQ, D_MODEL), F32),
        scratch_shapes=[
            pltpu.VMEM((D_MODEL, PEER_TT), F32),
            pltpu.VMEM((PEER_HEADS, PEER_N_KEYS, PEER_TT), BF16),
            pltpu.VMEM((PEER_HEADS, PEER_N_KEYS, PEER_TT), BF16),
            pltpu.VMEM((PEER_GROUP, PEER_TT), F32),
            pltpu.VMEM((PEER_GROUP, PEER_TT), F32),
            pltpu.VMEM((PEER_GROUP, PEER_TT), BF16),
            pltpu.VMEM((PEER_GROUP, PEER_TT), BF16),
        ],
        compiler_params=pltpu.CompilerParams(
            dimension_semantics=("arbitrary", "arbitrary"), vmem_limit_bytes=VMEM_LIMIT),
        name="peer",
    )(x1T, x1Tb, u_b, vT_b, s2T, v2, cnt, c1, g2, b2)


def _lane_replicated(p):
    return jnp.broadcast_to(p.astype(F32)[:, None], (p.shape[0], LANES))


def _transpose_cast_kernel(x_ref, o_ref):
    o_ref[...] = x_ref[...].T.astype(BF16)


def _transpose_cast(w):
    rows, cols = w.shape
    return pl.pallas_call(
        _transpose_cast_kernel,
        grid=(rows // XPOSE_ROWS,),
        in_specs=[pl.BlockSpec((XPOSE_ROWS, cols), lambda i: (i, 0))],
        out_specs=pl.BlockSpec((cols, XPOSE_ROWS), lambda i: (0, i)),
        out_shape=jax.ShapeDtypeStruct((cols, rows), BF16),
        compiler_params=pltpu.CompilerParams(
            dimension_semantics=("arbitrary",), vmem_limit_bytes=VMEM_LIMIT),
        name="transpose_cast",
    )(w)


def kernel(x, w_in, na_rpb, sg_norm_g, sg_norm_b, sg_spatial_w, sg_spatial_b, w_branch_na,
           w_branch_sg, w_out, ln1_g, ln1_b, peer_wq, peer_subkeys1, peer_subkeys2, peer_u, peer_v,
           ln2_g, ln2_b):
    assert x.shape == (1, SEQ, D_MODEL) and w_in.shape[0] == 1
    l = 0
    xs = x[0]
    q, k, v, gate_a, sg_part = _inproj(
        xs,
        w_in[l].astype(BF16),
        sg_norm_g[l][None, :], sg_norm_b[l][None, :],
        sg_spatial_w[l].astype(BF16),
        jnp.repeat(sg_spatial_b[l].T, SG_GROUP_DIM, axis=1),
        w_branch_sg[l].astype(BF16),
    )
    y_na = _natten(q, k, v, _natten_bias_table(na_rpb[l]))
    x1T, x1Tb = _mix(
        xs, y_na, gate_a, sg_part,
        w_branch_na[l].astype(BF16),
        _transpose_cast(w_out[l]),
        _lane_replicated(ln1_g[l]), _lane_replicated(ln1_b[l]),
    )
    cnt, c1, v2, s2T = _route(
        x1Tb, _transpose_cast(peer_wq[l]),
        peer_subkeys1[l].astype(BF16), peer_subkeys2[l].astype(BF16),
    )
    out = _peer(
        x1T, x1Tb,
        peer_u[l].astype(BF16), _transpose_cast(peer_v[l]),
        s2T, v2, cnt, c1,
        _lane_replicated(ln2_g[l]), _lane_replicated(ln2_b[l]),
    )
    return out[None]
```

```python
import functools
import math

import numpy as np
import jax
import jax.numpy as jnp
from jax import lax
from jax.experimental import pallas as pl
from jax.experimental.pallas import tpu as pltpu

F32 = jnp.float32
BF16 = jnp.bfloat16

D_MODEL = 1024
SEQ = 16384
GRID_W = 64
GRID_ROWS = SEQ // GRID_W
WIN_H = 8
WIN_W = 16
NA_HEADS = 8
NA_HEAD_DIM = 64
NA_WIDTH = NA_HEADS * NA_HEAD_DIM
SG_GROUPS = 4
SG_GROUP_DIM = 128
SG_WIDTH = SG_GROUPS * SG_GROUP_DIM
SG_CHUNK = 128
PEER_HEADS = 8
PEER_N_KEYS = 128
PEER_N_EXPERTS = PEER_N_KEYS * PEER_N_KEYS
PEER_KEY_DIM = 256
PEER_HALF = PEER_KEY_DIM // 2
PEER_TOPK = 16
DN_ALPHA = 2.0 ** 0.25
LN_EPS = 1e-5
MASK_BIAS = -1e30
GELU_C0 = math.sqrt(2.0 / math.pi)
GELU_C1 = 0.044715 * GELU_C0

LANES = 128
SUBLANES = 8

INPROJ_TM = 512
NA_ROWS = 4
MIX_TM = 512
ROUTE_TT = 512
XPOSE_ROWS = 512
PEER_TT = 512
PEER_ET = 2048
PEER_GROUP = 512
PEER_CHUNK_ROWS = 128
VMEM_LIMIT = 56 * 1024 * 1024


def _gelu(x):
    return jax.nn.gelu(x)


def _gelu_times(x, g):
    u = x * (GELU_C0 + GELU_C1 * (x * x))
    t = jnp.tanh(u.astype(BF16))
    return (0.5 * x.astype(BF16)) * (1.0 + t) * g


def _layer_norm_rows(x, g, b):
    mu = jnp.mean(x, axis=-1, keepdims=True)
    xc = x - mu
    var = jnp.mean(xc * xc, axis=-1, keepdims=True)
    return xc * lax.rsqrt(var + LN_EPS) * g + b


def _layer_norm_cols(z, g, b):
    mu = jnp.mean(z, axis=0, keepdims=True)
    zc = z - mu
    var = jnp.mean(zc * zc, axis=0, keepdims=True)
    return zc * lax.rsqrt(var + LN_EPS) * g + b


def _tile_lanes(p128, width):
    reps = width // LANES
    return p128 if reps == 1 else jnp.concatenate([p128] * reps, axis=1)


def _inproj_kernel(x_ref, w_ref, ng_ref, nb_ref, ws_ref, bs_ref, wsg_ref,
                   q_ref, k_ref, v_ref, ga_ref, sg_ref):
    xb = x_ref[...].astype(BF16)

    def proj(lo, hi):
        return jnp.dot(xb, w_ref[:, lo:hi], preferred_element_type=F32)

    o = 0
    q_ref[...] = (proj(o, o + NA_WIDTH) * (NA_HEAD_DIM ** -0.5)).astype(BF16)
    o += NA_WIDTH
    k_ref[...] = proj(o, o + NA_WIDTH).astype(BF16)
    o += NA_WIDTH
    v_ref[...] = proj(o, o + NA_WIDTH).astype(BF16)
    o += NA_WIDTH
    u = _gelu(proj(o, o + SG_WIDTH))
    o += SG_WIDTH
    vn = _layer_norm_rows(_gelu(proj(o, o + SG_WIDTH)), ng_ref[...], nb_ref[...]).astype(BF16)
    o += SG_WIDTH
    ga_ref[...] = jax.nn.sigmoid(proj(o, o + D_MODEL))
    o += D_MODEL
    gate_b = jax.nn.sigmoid(proj(o, o + D_MODEL))

    chunks = []
    for c in range(INPROJ_TM // SG_CHUNK):
        vc = vn[c * SG_CHUNK:(c + 1) * SG_CHUNK, :]
        groups = [
            jnp.dot(ws_ref[g], vc[:, g * SG_GROUP_DIM:(g + 1) * SG_GROUP_DIM],
                    preferred_element_type=F32)
            for g in range(SG_GROUPS)
        ]
        chunks.append(jnp.concatenate(groups, axis=1) + bs_ref[...])
    y_sg = (u * jnp.concatenate(chunks, axis=0)).astype(BF16)
    sg_ref[...] = gate_b * jnp.dot(y_sg, wsg_ref[...], preferred_element_type=F32)


def _inproj(x, w_in_b, ng, nb, ws_b, bs_full, wsg_b):
    n_steps = SEQ // INPROJ_TM
    in_cols = w_in_b.shape[1]
    tok = lambda width: pl.BlockSpec((INPROJ_TM, width), lambda i: (i, 0))
    full = lambda shape: pl.BlockSpec(shape, lambda i: (0,) * len(shape))
    return pl.pallas_call(
        _inproj_kernel,
        grid=(n_steps,),
        in_specs=[
            tok(D_MODEL),
            full((D_MODEL, in_cols)),
            full((1, SG_WIDTH)),
            full((1, SG_WIDTH)),
            full((SG_GROUPS, SG_CHUNK, SG_CHUNK)),
            full((SG_CHUNK, SG_WIDTH)),
            full((SG_WIDTH, D_MODEL)),
        ],
        out_specs=[tok(NA_WIDTH), tok(NA_WIDTH), tok(NA_WIDTH), tok(D_MODEL), tok(D_MODEL)],
        out_shape=[
            jax.ShapeDtypeStruct((SEQ, NA_WIDTH), BF16),
            jax.ShapeDtypeStruct((SEQ, NA_WIDTH), BF16),
            jax.ShapeDtypeStruct((SEQ, NA_WIDTH), BF16),
            jax.ShapeDtypeStruct((SEQ, D_MODEL), F32),
            jax.ShapeDtypeStruct((SEQ, D_MODEL), F32),
        ],
        compiler_params=pltpu.CompilerParams(
            dimension_semantics=("arbitrary",), vmem_limit_bytes=VMEM_LIMIT),
        name="inproj",
    )(x, w_in_b, ng, nb, ws_b, bs_full, wsg_b)


def _natten_kernel(q_ref, kp_ref, kc_ref, kn_ref, vp_ref, vc_ref, vn_ref, bias_ref,
                   o_ref, kwin_ref, vwin_ref):
    i = pl.program_id(0)
    last = pl.num_programs(0) - 1
    kwin_ref[0:NA_ROWS] = kp_ref[...]
    kwin_ref[NA_ROWS:2 * NA_ROWS] = kc_ref[...]
    kwin_ref[2 * NA_ROWS:3 * NA_ROWS] = kn_ref[...]
    vwin_ref[0:NA_ROWS] = vp_ref[...]
    vwin_ref[NA_ROWS:2 * NA_ROWS] = vc_ref[...]
    vwin_ref[2 * NA_ROWS:3 * NA_ROWS] = vn_ref[...]

    n_keys = WIN_H * GRID_W
    lane_head = lax.broadcasted_iota(jnp.int32, (GRID_W, NA_WIDTH), 1) // NA_HEAD_DIM
    for j in range(NA_ROWS):
        off = jnp.where(i == 0, NA_ROWS, jnp.where(i == last, 0, j))
        dneg = jnp.where(i == 0, j, jnp.where(i == last, WIN_H // 2 + j, WIN_H // 2))
        kw = kwin_ref[pl.ds(off, WIN_H)].reshape(n_keys, NA_WIDTH)
        vw = vwin_ref[pl.ds(off, WIN_H)].reshape(n_keys, NA_WIDTH)
        qj = q_ref[j]
        qm = jnp.concatenate(
            [jnp.where(lane_head == h, qj, jnp.zeros_like(qj)) for h in range(NA_HEADS)], axis=0)
        logits = lax.dot_general(qm, kw, (((1,), (1,)), ((), ())), preferred_element_type=F32)
        logits = logits + bias_ref[dneg]
        m = jnp.max(logits, axis=-1, keepdims=True)
        p = jnp.exp(logits - m)
        inv = 1.0 / jnp.sum(p, axis=-1, keepdims=True)
        r = jnp.dot(p.astype(BF16), vw, preferred_element_type=F32) * inv
        acc = jnp.zeros((GRID_W, NA_WIDTH), F32)
        for h in range(NA_HEADS):
            acc = acc + jnp.where(lane_head == h, r[h * GRID_W:(h + 1) * GRID_W, :], 0.0)
        o_ref[j] = acc.astype(BF16)


def _natten(q, k, v, bias_tab):
    n_steps = GRID_ROWS // NA_ROWS
    q3 = q.reshape(GRID_ROWS, GRID_W, NA_WIDTH)
    k3 = k.reshape(GRID_ROWS, GRID_W, NA_WIDTH)
    v3 = v.reshape(GRID_ROWS, GRID_W, NA_WIDTH)
    blk = (NA_ROWS, GRID_W, NA_WIDTH)
    cur = pl.BlockSpec(blk, lambda i: (i, 0, 0))
    prev = pl.BlockSpec(blk, lambda i: (jnp.maximum(i - 1, 0), 0, 0))
    nxt = pl.BlockSpec(blk, lambda i: (jnp.minimum(i + 1, n_steps - 1), 0, 0))
    out = pl.pallas_call(
        _natten_kernel,
        grid=(n_steps,),
        in_specs=[cur, prev, cur, nxt, prev, cur, nxt,
                  pl.BlockSpec(bias_tab.shape, lambda i: (0, 0, 0))],
        out_specs=cur,
        out_shape=jax.ShapeDtypeStruct((GRID_ROWS, GRID_W, NA_WIDTH), BF16),
        scratch_shapes=[pltpu.VMEM((3 * NA_ROWS, GRID_W, NA_WIDTH), BF16),
                        pltpu.VMEM((3 * NA_ROWS, GRID_W, NA_WIDTH), BF16)],
        compiler_params=pltpu.CompilerParams(
            dimension_semantics=("arbitrary",), vmem_limit_bytes=VMEM_LIMIT),
        name="natten",
    )(q3, k3, k3, k3, v3, v3, v3, bias_tab)
    return out.reshape(SEQ, NA_WIDTH)


def _natten_bias_table(rpb):
    cols = np.arange(GRID_W)
    col_start = np.clip(cols - WIN_W // 2, 0, GRID_W - WIN_W)
    kc = np.arange(GRID_W)
    valid = (kc[None, :] >= col_start[:, None]) & (kc[None, :] < col_start[:, None] + WIN_W)
    dc = np.where(valid, kc[None, :] - cols[:, None] + (WIN_W - 1), -1)
    dr = np.arange(WIN_H)[None, :] + (WIN_H - 1) - np.arange(WIN_H)[:, None]
    rows = jnp.transpose(rpb[:, dr], (1, 0, 2, 3)).astype(F32)
    onehot = (dc[None, :, :] == np.arange(2 * WIN_W - 1)[:, None, None]).astype(np.float32)
    table = jnp.einsum('xhkd,dcz->xhckz', rows, jnp.asarray(onehot),
                       precision=lax.Precision.HIGHEST)
    table = jnp.where(jnp.asarray(valid)[None, None, :, None, :], table, MASK_BIAS)
    return table.reshape(WIN_H, NA_HEADS * GRID_W, WIN_H * GRID_W)


def _mix_kernel(x_ref, yna_ref, ga_ref, sg_ref, wna_ref, woutT_ref, g1_ref, b1_ref,
                x1T_ref, x1Tb_ref):
    merged = ga_ref[...] * jnp.dot(yna_ref[...], wna_ref[...], preferred_element_type=F32)
    merged = (merged + sg_ref[...]).astype(BF16)
    outT = lax.dot_general(woutT_ref[...], merged, (((1,), (1,)), ((), ())),
                           preferred_element_type=F32)
    z = DN_ALPHA * x_ref[...].T + outT
    x1T = _layer_norm_cols(z, _tile_lanes(g1_ref[...], MIX_TM), _tile_lanes(b1_ref[...], MIX_TM))
    x1T_ref[...] = x1T
    x1Tb_ref[...] = x1T.astype(BF16)


def _mix(x, y_na, gate_a, sg_part, wna_b, woutT_b, g1, b1):
    n_steps = SEQ // MIX_TM
    tok = lambda width: pl.BlockSpec((MIX_TM, width), lambda i: (i, 0))
    full = lambda shape: pl.BlockSpec(shape, lambda i: (0,) * len(shape))
    featT = pl.BlockSpec((D_MODEL, MIX_TM), lambda i: (0, i))
    return pl.pallas_call(
        _mix_kernel,
        grid=(n_steps,),
        in_specs=[
            tok(D_MODEL), tok(NA_WIDTH), tok(D_MODEL), tok(D_MODEL),
            full((NA_WIDTH, D_MODEL)), full((D_MODEL, D_MODEL)),
            full((D_MODEL, LANES)), full((D_MODEL, LANES)),
        ],
        out_specs=[featT, featT],
        out_shape=[
            jax.ShapeDtypeStruct((D_MODEL, SEQ), F32),
            jax.ShapeDtypeStruct((D_MODEL, SEQ), BF16),
        ],
        compiler_params=pltpu.CompilerParams(
            dimension_semantics=("arbitrary",), vmem_limit_bytes=VMEM_LIMIT),
        name="mix",
    )(x, y_na, gate_a, sg_part, wna_b, woutT_b, g1, b1)


def _cmpx(xs, i, j):
    hi = jnp.maximum(xs[i], xs[j])
    lo = jnp.minimum(xs[i], xs[j])
    xs[i], xs[j] = hi, lo


def _bitonic_sort_desc(xs):
    xs = list(xs)
    n = len(xs)
    size = 2
    while size <= n:
        stride = size // 2
        while stride >= 1:
            for i in range(n):
                l = i ^ stride
                if l > i:
                    if (i & size) == 0:
                        _cmpx(xs, i, l)
                    else:
                        _cmpx(xs, l, i)
            stride //= 2
        size *= 2
    return xs


def _bitonic_merge_desc(xs):
    xs = list(xs)
    n = len(xs)
    stride = n // 2
    while stride >= 1:
        for i in range(n):
            if (i & stride) == 0:
                _cmpx(xs, i, i + stride)
        stride //= 2
    return xs


def _top_merge(xs, ys):
    n = len(xs)
    return _bitonic_merge_desc([jnp.maximum(xs[i], ys[n - 1 - i]) for i in range(n)])


def _top16_of(vals):
    k = PEER_TOPK
    pad = (-len(vals)) % k
    vals = list(vals) + [jnp.full_like(vals[0], -jnp.inf)] * pad
    groups = [_bitonic_sort_desc(vals[g:g + k]) for g in range(0, len(vals), k)]
    top = groups[0]
    for grp in groups[1:]:
        top = _top_merge(top, grp)
    return top


def _route_kernel(x1Tb_ref, wqT_ref, k1_ref, k2_ref,
                  cnt_ref, c1_ref, v2_ref, s2_ref, s1k_ref, s2k_ref):
    qpT = jnp.dot(wqT_ref[...], x1Tb_ref[...], preferred_element_type=F32).astype(BF16)
    for h in range(PEER_HEADS):
        lo = h * PEER_KEY_DIM
        s1 = jnp.dot(k1_ref[...], qpT[lo:lo + PEER_HALF, :], preferred_element_type=F32)
        s2 = jnp.dot(k2_ref[...], qpT[lo + PEER_HALF:lo + PEER_KEY_DIM, :],
                     preferred_element_type=F32)
        s2_ref[h] = s2
        for slab in range(ROUTE_TT // LANES):
            lanes = slice(slab * LANES, (slab + 1) * LANES)
            key_rows = pl.ds(h, PEER_N_KEYS, stride=PEER_HEADS)
            s1k_ref[slab, key_rows, :] = s1[:, lanes]
            s2k_ref[slab, key_rows, :] = s2[:, lanes]

    def route_slab(slab, carry):
        def key(ref, a):
            return ref[slab, SUBLANES * a:SUBLANES * (a + 1), :]

        v1 = _top16_of([key(s1k_ref, a) for a in range(PEER_N_KEYS)])
        v2 = _top16_of([key(s2k_ref, b) for b in range(PEER_N_KEYS)])
        cands = [v1[i] + v2[j] for i in range(PEER_TOPK) for j in range(PEER_TOPK)
                 if (i + 1) * (j + 1) <= PEER_TOPK]
        top = _top16_of(cands)
        tau = top[PEER_TOPK - 1]
        z = jnp.ones_like(tau)
        for t in top[1:]:
            z = z + jnp.exp(t - top[0])
        inv_z = 1.0 / z
        need = []
        for j in range(PEER_TOPK):
            v2_ref[slab, SUBLANES * j:SUBLANES * (j + 1), :] = v2[j]
            nj = jnp.full((SUBLANES, LANES), jnp.inf, F32)
            for i in range(PEER_TOPK):
                nj = jnp.where(v1[i] + v2[j] >= tau, v1[i], nj)
            need.append(nj)
        for a in range(PEER_N_KEYS):
            s1a = key(s1k_ref, a)
            cnt = jnp.zeros((SUBLANES, LANES), F32)
            for j in range(PEER_TOPK):
                cnt = jnp.where(s1a >= need[j], float(j + 1), cnt)
            cnt_ref[slab, SUBLANES * a:SUBLANES * (a + 1), :] = cnt
            c1_ref[slab, SUBLANES * a:SUBLANES * (a + 1), :] = jnp.exp(s1a - v1[0]) * inv_z
        return carry

    lax.fori_loop(0, ROUTE_TT // LANES, route_slab, 0)


def _route(x1Tb, wqT_b, k1_b, k2_b):
    n_steps = SEQ // ROUTE_TT
    slabs = ROUTE_TT // LANES
    key_rows = PEER_N_KEYS * PEER_HEADS
    top_rows = PEER_TOPK * PEER_HEADS
    full = lambda shape: pl.BlockSpec(shape, lambda i: (0,) * len(shape))
    key_major = pl.BlockSpec((slabs, key_rows, LANES), lambda i: (i, 0, 0))
    key_major_shape = jax.ShapeDtypeStruct((SEQ // LANES, key_rows, LANES), F32)
    return pl.pallas_call(
        _route_kernel,
        grid=(n_steps,),
        in_specs=[
            pl.BlockSpec((D_MODEL, ROUTE_TT), lambda i: (0, i)),
            full((PEER_HEADS * PEER_KEY_DIM, D_MODEL)),
            full((PEER_N_KEYS, PEER_HALF)), full((PEER_N_KEYS, PEER_HALF)),
        ],
        out_specs=[
            key_major, key_major,
            pl.BlockSpec((slabs, top_rows, LANES), lambda i: (i, 0, 0)),
            pl.BlockSpec((PEER_HEADS, PEER_N_KEYS, ROUTE_TT), lambda i: (0, 0, i)),
        ],
        out_shape=[
            key_major_shape, key_major_shape,
            jax.ShapeDtypeStruct((SEQ // LANES, top_rows, LANES), F32),
            jax.ShapeDtypeStruct((PEER_HEADS, PEER_N_KEYS, SEQ), F32),
        ],
        scratch_shapes=[pltpu.VMEM((slabs, key_rows, LANES), F32),
                        pltpu.VMEM((slabs, key_rows, LANES), F32)],
        compiler_params=pltpu.CompilerParams(
            dimension_semantics=("arbitrary",), vmem_limit_bytes=VMEM_LIMIT),
        name="route",
    )(x1Tb, wqT_b, k1_b, k2_b)


def _peer_kernel(x1T_ref, x1Tb_ref, u_ref, vT_ref, s2_ref, v2_ref, cnt_ref, c1_ref, g2_ref, b2_ref,
                 o_ref, acc_ref, p2_ref, r2_ref, a0_ref, a1_ref, w0_ref, w1_ref):
    e = pl.program_id(1)

    @pl.when(e == 0)
    def _():
        acc_ref[...] = jnp.zeros_like(acc_ref)
        for h in range(PEER_HEADS):
            for lt in range(PEER_TT // LANES):
                lanes = slice(lt * LANES, (lt + 1) * LANES)
                s2h = s2_ref[h, :, lanes]
                rank = jnp.zeros_like(s2h)
                for j in range(PEER_TOPK):
                    row = j * PEER_HEADS + h
                    rank = jnp.where(v2_ref[lt, row:row + 1, :] > s2h, float(j + 1), rank)
                r2_ref[h, :, lanes] = rank.astype(BF16)
                p2_ref[h, :, lanes] = jnp.exp(s2h - v2_ref[lt, h:h + 1, :]).astype(BF16)

    blocks_per_tile = PEER_ET // PEER_N_KEYS
    n_groups = PEER_ET // PEER_GROUP
    blocks_per_group = PEER_GROUP // PEER_N_KEYS
    a_bufs = (a0_ref, a1_ref)
    w_bufs = (w0_ref, w1_ref)

    def scores(grp):
        rows = slice(grp * PEER_GROUP, (grp + 1) * PEER_GROUP)
        return jnp.dot(u_ref[rows, :], x1Tb_ref[...], preferred_element_type=F32)

    def contract(grp):
        cols = slice(grp * PEER_GROUP, (grp + 1) * PEER_GROUP)
        acc_ref[...] += jnp.dot(vT_ref[0, :, cols], w_bufs[grp % 2][...],
                                preferred_element_type=F32)

    def gate(grp):
        a_ref, w_ref = a_bufs[grp % 2], w_bufs[grp % 2]
        for blk in range(blocks_per_group):
            a = e * blocks_per_tile + grp * blocks_per_group + blk
            head_rows = pl.ds(pl.multiple_of(a * PEER_HEADS, PEER_HEADS), PEER_HEADS)
            for lt in range(PEER_TT // LANES):
                lanes = slice(lt * LANES, (lt + 1) * LANES)
                cnt_heads = cnt_ref[lt, head_rows, :].astype(BF16)
                c1_heads = c1_ref[lt, head_rows, :].astype(BF16)
                for rc in range(PEER_N_KEYS // PEER_CHUNK_ROWS):
                    rows = slice(rc * PEER_CHUNK_ROWS, (rc + 1) * PEER_CHUNK_ROWS)
                    g = None
                    for h in range(PEER_HEADS):
                        cnt = cnt_heads[h:h + 1, :]
                        c1 = c1_heads[h:h + 1, :]
                        term = jnp.where(r2_ref[h, rows, lanes] < cnt,
                                         p2_ref[h, rows, lanes] * c1, jnp.zeros((), BF16))
                        g = term if g is None else g + term
                    out_rows = slice(blk * PEER_N_KEYS + rc * PEER_CHUNK_ROWS,
                                     blk * PEER_N_KEYS + (rc + 1) * PEER_CHUNK_ROWS)
                    w_ref[out_rows, lanes] = _gelu_times(a_ref[out_rows, lanes], g)

    a_bufs[0][...] = scores(0)
    for grp in range(n_groups):
        if grp + 1 < n_groups:
            a_bufs[(grp + 1) % 2][...] = scores(grp + 1)
        gate(grp)
        if grp > 0:
            contract(grp - 1)
    contract(n_groups - 1)

    @pl.when(e == pl.num_programs(1) - 1)
    def _():
        z = DN_ALPHA * x1T_ref[...] + acc_ref[...]
        y = _layer_norm_cols(z, _tile_lanes(g2_ref[...], PEER_TT), _tile_lanes(b2_ref[...], PEER_TT))
        o_ref[...] = y.T


def _peer(x1T, x1Tb, u_b, vT_b, s2T, v2, cnt, c1, g2, b2):
    n_tok = SEQ // PEER_TT
    n_exp = PEER_N_EXPERTS // PEER_ET
    featT = pl.BlockSpec((D_MODEL, PEER_TT), lambda j, e: (0, j))
    side = pl.BlockSpec((PEER_HEADS, PEER_N_KEYS, PEER_TT), lambda j, e: (0, 0, j))
    key_major = pl.BlockSpec((PEER_TT // LANES, PEER_N_KEYS * PEER_HEADS, LANES),
                             lambda j, e: (j, 0, 0))
    par = pl.BlockSpec((D_MODEL, LANES), lambda j, e: (0, 0))
    return pl.pallas_call(
        _peer_kernel,
        grid=(n_tok, n_exp),
        in_specs=[
            featT, featT,
            pl.BlockSpec((PEER_ET, D_MODEL), lambda j, e: (e, 0)),
            pl.BlockSpec((1, D_MODEL, PEER_ET), lambda j, e: (e, 0, 0)),
            side,
            pl.BlockSpec((PEER_TT // LANES, PEER_TOPK * PEER_HEADS, LANES), lambda j, e: (j, 0, 0)),
            key_major, key_major, par, par,
        ],
        out_specs=pl.BlockSpec((PEER_TT, D_MODEL), lambda j, e: (j, 0)),
        out_shape=jax.ShapeDtypeStruct((SEQ, D_MODEL), F32),
        scratch_shapes=[
            pltpu.VMEM((D_MODEL, PEER_TT), F32),
            pltpu.VMEM((PEER_HEADS, PEER_N_KEYS, PEER_TT), BF16),
            pltpu.VMEM((PEER_HEADS, PEER_N_KEYS, PEER_TT), BF16),
            pltpu.VMEM((PEER_GROUP, PEER_TT), F32),
            pltpu.VMEM((PEER_GROUP, PEER_TT), F32),
            pltpu.VMEM((PEER_GROUP, PEER_TT), BF16),
            pltpu.VMEM((PEER_GROUP, PEER_TT), BF16),
        ],
        compiler_params=pltpu.CompilerParams(
            dimension_semantics=("arbitrary", "arbitrary"), vmem_limit_bytes=VMEM_LIMIT),
        name="peer",
    )(x1T, x1Tb, u_b, vT_b, s2T, v2, cnt, c1, g2, b2)


def _lane_replicated(p):
    return jnp.broadcast_to(p.astype(F32)[:, None], (p.shape[0], LANES))


def _transpose_cast_kernel(x_ref, o_ref):
    o_ref[0] = x_ref[...].T.astype(BF16)


def _transpose_cast(w, tile):
    rows, cols = w.shape
    per_tile = tile // XPOSE_ROWS
    return pl.pallas_call(
        _transpose_cast_kernel,
        grid=(rows // XPOSE_ROWS,),
        in_specs=[pl.BlockSpec((XPOSE_ROWS, cols), lambda i: (i, 0))],
        out_specs=pl.BlockSpec((1, cols, XPOSE_ROWS), lambda i: (i // per_tile, 0, i % per_tile)),
        out_shape=jax.ShapeDtypeStruct((rows // tile, cols, tile), BF16),
        compiler_params=pltpu.CompilerParams(
            dimension_semantics=("arbitrary",), vmem_limit_bytes=VMEM_LIMIT),
        name="transpose_cast",
    )(w)


def kernel(x, w_in, na_rpb, sg_norm_g, sg_norm_b, sg_spatial_w, sg_spatial_b, w_branch_na,
           w_branch_sg, w_out, ln1_g, ln1_b, peer_wq, peer_subkeys1, peer_subkeys2, peer_u, peer_v,
           ln2_g, ln2_b):
    assert x.shape == (1, SEQ, D_MODEL) and w_in.shape[0] == 1
    l = 0
    xs = x[0]
    q, k, v, gate_a, sg_part = _inproj(
        xs,
        w_in[l].astype(BF16),
        sg_norm_g[l][None, :], sg_norm_b[l][None, :],
        sg_spatial_w[l].astype(BF16),
        jnp.repeat(sg_spatial_b[l].T, SG_GROUP_DIM, axis=1),
        w_branch_sg[l].astype(BF16),
    )
    y_na = _natten(q, k, v, _natten_bias_table(na_rpb[l]))
    x1T, x1Tb = _mix(
        xs, y_na, gate_a, sg_part,
        w_branch_na[l].astype(BF16),
        _transpose_cast(w_out[l], D_MODEL)[0],
        _lane_replicated(ln1_g[l]), _lane_replicated(ln1_b[l]),
    )
    cnt, c1, v2, s2T = _route(
        x1Tb, _transpose_cast(peer_wq[l], D_MODEL)[0],
        peer_subkeys1[l].astype(BF16), peer_subkeys2[l].astype(BF16),
    )
    out = _peer(
        x1T, x1Tb,
        peer_u[l].astype(BF16), _transpose_cast(peer_v[l], PEER_ET),
        s2T, v2, cnt, c1,
        _lane_replicated(ln2_g[l]), _lane_replicated(ln2_b[l]),
    )
    return out[None]
```

```python
import functools
import math

import numpy as np
import jax
import jax.numpy as jnp
from jax import lax
from jax.experimental import pallas as pl
from jax.experimental.pallas import tpu as pltpu

F32 = jnp.float32
BF16 = jnp.bfloat16

D_MODEL = 1024
SEQ = 16384
GRID_W = 64
GRID_ROWS = SEQ // GRID_W
WIN_H = 8
WIN_W = 16
NA_HEADS = 8
NA_HEAD_DIM = 64
NA_WIDTH = NA_HEADS * NA_HEAD_DIM
SG_GROUPS = 4
SG_GROUP_DIM = 128
SG_WIDTH = SG_GROUPS * SG_GROUP_DIM
SG_CHUNK = 128
PEER_HEADS = 8
PEER_N_KEYS = 128
PEER_N_EXPERTS = PEER_N_KEYS * PEER_N_KEYS
PEER_KEY_DIM = 256
PEER_HALF = PEER_KEY_DIM // 2
PEER_TOPK = 16
DN_ALPHA = 2.0 ** 0.25
LN_EPS = 1e-5
MASK_BIAS = -1e30
PEER_NO_MATCH = 2.0
GELU_C0 = math.sqrt(2.0 / math.pi)
GELU_C1 = 0.044715 * GELU_C0

LANES = 128
SUBLANES = 8

INPROJ_TM = 512
NA_ROWS = 4
MIX_TM = 512
ROUTE_TT = 512
XPOSE_ROWS = 512
PEER_TT = 512
PEER_ET = 2048
PEER_GROUP = 512
PEER_CHUNK_ROWS = 64
VMEM_LIMIT = 56 * 1024 * 1024


def _gelu(x):
    return jax.nn.gelu(x)


def _gelu_times(x, g):
    u = x * (GELU_C0 + GELU_C1 * (x * x))
    return (x * (1.0 + jnp.tanh(u)) * g).astype(BF16)


def _layer_norm_rows(x, g, b):
    mu = jnp.mean(x, axis=-1, keepdims=True)
    xc = x - mu
    var = jnp.mean(xc * xc, axis=-1, keepdims=True)
    return xc * lax.rsqrt(var + LN_EPS) * g + b


def _layer_norm_cols(z, g, b):
    mu = jnp.mean(z, axis=0, keepdims=True)
    zc = z - mu
    var = jnp.mean(zc * zc, axis=0, keepdims=True)
    return zc * lax.rsqrt(var + LN_EPS) * g + b


def _tile_lanes(p128, width):
    reps = width // LANES
    return p128 if reps == 1 else jnp.concatenate([p128] * reps, axis=1)


def _inproj_kernel(x_ref, w_ref, ng_ref, nb_ref, ws_ref, bs_ref, wsg_ref,
                   q_ref, k_ref, v_ref, ga_ref, sg_ref):
    xb = x_ref[...].astype(BF16)

    def proj(lo, hi):
        return jnp.dot(xb, w_ref[:, lo:hi], preferred_element_type=F32)

    o = 0
    q_ref[...] = (proj(o, o + NA_WIDTH) * (NA_HEAD_DIM ** -0.5)).astype(BF16)
    o += NA_WIDTH
    k_ref[...] = proj(o, o + NA_WIDTH).astype(BF16)
    o += NA_WIDTH
    v_ref[...] = proj(o, o + NA_WIDTH).astype(BF16)
    o += NA_WIDTH
    u = _gelu(proj(o, o + SG_WIDTH))
    o += SG_WIDTH
    vn = _layer_norm_rows(_gelu(proj(o, o + SG_WIDTH)), ng_ref[...], nb_ref[...]).astype(BF16)
    o += SG_WIDTH
    ga_ref[...] = jax.nn.sigmoid(proj(o, o + D_MODEL))
    o += D_MODEL
    gate_b = jax.nn.sigmoid(proj(o, o + D_MODEL))

    chunks = []
    for c in range(INPROJ_TM // SG_CHUNK):
        vc = vn[c * SG_CHUNK:(c + 1) * SG_CHUNK, :]
        groups = [
            jnp.dot(ws_ref[g], vc[:, g * SG_GROUP_DIM:(g + 1) * SG_GROUP_DIM],
                    preferred_element_type=F32)
            for g in range(SG_GROUPS)
        ]
        chunks.append(jnp.concatenate(groups, axis=1) + bs_ref[...])
    y_sg = (u * jnp.concatenate(chunks, axis=0)).astype(BF16)
    sg_ref[...] = gate_b * jnp.dot(y_sg, wsg_ref[...], preferred_element_type=F32)


def _inproj(x, w_in_b, ng, nb, ws_b, bs_full, wsg_b):
    n_steps = SEQ // INPROJ_TM
    in_cols = w_in_b.shape[1]
    tok = lambda width: pl.BlockSpec((INPROJ_TM, width), lambda i: (i, 0))
    full = lambda shape: pl.BlockSpec(shape, lambda i: (0,) * len(shape))
    return pl.pallas_call(
        _inproj_kernel,
        grid=(n_steps,),
        in_specs=[
            tok(D_MODEL),
            full((D_MODEL, in_cols)),
            full((1, SG_WIDTH)),
            full((1, SG_WIDTH)),
            full((SG_GROUPS, SG_CHUNK, SG_CHUNK)),
            full((SG_CHUNK, SG_WIDTH)),
            full((SG_WIDTH, D_MODEL)),
        ],
        out_specs=[tok(NA_WIDTH), tok(NA_WIDTH), tok(NA_WIDTH), tok(D_MODEL), tok(D_MODEL)],
        out_shape=[
            jax.ShapeDtypeStruct((SEQ, NA_WIDTH), BF16),
            jax.ShapeDtypeStruct((SEQ, NA_WIDTH), BF16),
            jax.ShapeDtypeStruct((SEQ, NA_WIDTH), BF16),
            jax.ShapeDtypeStruct((SEQ, D_MODEL), F32),
            jax.ShapeDtypeStruct((SEQ, D_MODEL), F32),
        ],
        compiler_params=pltpu.CompilerParams(
            dimension_semantics=("arbitrary",), vmem_limit_bytes=VMEM_LIMIT),
        name="inproj",
    )(x, w_in_b, ng, nb, ws_b, bs_full, wsg_b)


def _natten_kernel(q_ref, kp_ref, kc_ref, kn_ref, vp_ref, vc_ref, vn_ref, bias_ref,
                   o_ref, kwin_ref, vwin_ref):
    i = pl.program_id(0)
    last = pl.num_programs(0) - 1
    kwin_ref[0:NA_ROWS] = kp_ref[...]
    kwin_ref[NA_ROWS:2 * NA_ROWS] = kc_ref[...]
    kwin_ref[2 * NA_ROWS:3 * NA_ROWS] = kn_ref[...]
    vwin_ref[0:NA_ROWS] = vp_ref[...]
    vwin_ref[NA_ROWS:2 * NA_ROWS] = vc_ref[...]
    vwin_ref[2 * NA_ROWS:3 * NA_ROWS] = vn_ref[...]

    n_keys = WIN_H * GRID_W
    lane_head = lax.broadcasted_iota(jnp.int32, (GRID_W, NA_WIDTH), 1) // NA_HEAD_DIM
    for j in range(NA_ROWS):
        off = jnp.where(i == 0, NA_ROWS, jnp.where(i == last, 0, j))
        dneg = jnp.where(i == 0, j, jnp.where(i == last, WIN_H // 2 + j, WIN_H // 2))
        kw = kwin_ref[pl.ds(off, WIN_H)].reshape(n_keys, NA_WIDTH)
        vw = vwin_ref[pl.ds(off, WIN_H)].reshape(n_keys, NA_WIDTH)
        qj = q_ref[j]
        qm = jnp.concatenate(
            [jnp.where(lane_head == h, qj, jnp.zeros_like(qj)) for h in range(NA_HEADS)], axis=0)
        logits = lax.dot_general(qm, kw, (((1,), (1,)), ((), ())), preferred_element_type=F32)
        logits = logits + bias_ref[dneg]
        m = jnp.max(logits, axis=-1, keepdims=True)
        p = jnp.exp(logits - m)
        inv = 1.0 / jnp.sum(p, axis=-1, keepdims=True)
        r = jnp.dot(p.astype(BF16), vw, preferred_element_type=F32) * inv
        acc = jnp.zeros((GRID_W, NA_WIDTH), F32)
        for h in range(NA_HEADS):
            acc = acc + jnp.where(lane_head == h, r[h * GRID_W:(h + 1) * GRID_W, :], 0.0)
        o_ref[j] = acc.astype(BF16)


def _natten(q, k, v, bias_tab):
    n_steps = GRID_ROWS // NA_ROWS
    q3 = q.reshape(GRID_ROWS, GRID_W, NA_WIDTH)
    k3 = k.reshape(GRID_ROWS, GRID_W, NA_WIDTH)
    v3 = v.reshape(GRID_ROWS, GRID_W, NA_WIDTH)
    blk = (NA_ROWS, GRID_W, NA_WIDTH)
    cur = pl.BlockSpec(blk, lambda i: (i, 0, 0))
    prev = pl.BlockSpec(blk, lambda i: (jnp.maximum(i - 1, 0), 0, 0))
    nxt = pl.BlockSpec(blk, lambda i: (jnp.minimum(i + 1, n_steps - 1), 0, 0))
    out = pl.pallas_call(
        _natten_kernel,
        grid=(n_steps,),
        in_specs=[cur, prev, cur, nxt, prev, cur, nxt,
                  pl.BlockSpec(bias_tab.shape, lambda i: (0, 0, 0))],
        out_specs=cur,
        out_shape=jax.ShapeDtypeStruct((GRID_ROWS, GRID_W, NA_WIDTH), BF16),
        scratch_shapes=[pltpu.VMEM((3 * NA_ROWS, GRID_W, NA_WIDTH), BF16),
                        pltpu.VMEM((3 * NA_ROWS, GRID_W, NA_WIDTH), BF16)],
        compiler_params=pltpu.CompilerParams(
            dimension_semantics=("arbitrary",), vmem_limit_bytes=VMEM_LIMIT),
        name="natten",
    )(q3, k3, k3, k3, v3, v3, v3, bias_tab)
    return out.reshape(SEQ, NA_WIDTH)


def _natten_bias_table(rpb):
    cols = np.arange(GRID_W)
    col_start = np.clip(cols - WIN_W // 2, 0, GRID_W - WIN_W)
    kc = np.arange(GRID_W)
    valid = (kc[None, :] >= col_start[:, None]) & (kc[None, :] < col_start[:, None] + WIN_W)
    dc = np.where(valid, kc[None, :] - cols[:, None] + (WIN_W - 1), -1)
    dr = np.arange(WIN_H)[None, :] + (WIN_H - 1) - np.arange(WIN_H)[:, None]
    rows = jnp.transpose(rpb[:, dr], (1, 0, 2, 3)).astype(F32)
    onehot = (dc[None, :, :] == np.arange(2 * WIN_W - 1)[:, None, None]).astype(np.float32)
    table = jnp.einsum('xhkd,dcz->xhckz', rows, jnp.asarray(onehot),
                       precision=lax.Precision.HIGHEST)
    table = jnp.where(jnp.asarray(valid)[None, None, :, None, :], table, MASK_BIAS)
    return table.reshape(WIN_H, NA_HEADS * GRID_W, WIN_H * GRID_W)


def _mix_kernel(x_ref, yna_ref, ga_ref, sg_ref, wna_ref, woutT_ref, g1_ref, b1_ref,
                x1T_ref, x1Tb_ref):
    merged = ga_ref[...] * jnp.dot(yna_ref[...], wna_ref[...], preferred_element_type=F32)
    merged = (merged + sg_ref[...]).astype(BF16)
    outT = lax.dot_general(woutT_ref[...], merged, (((1,), (1,)), ((), ())),
                           preferred_element_type=F32)
    z = DN_ALPHA * x_ref[...].T + outT
    x1T = _layer_norm_cols(z, _tile_lanes(g1_ref[...], MIX_TM), _tile_lanes(b1_ref[...], MIX_TM))
    x1T_ref[...] = x1T
    x1Tb_ref[...] = x1T.astype(BF16)


def _mix(x, y_na, gate_a, sg_part, wna_b, woutT_b, g1, b1):
    n_steps = SEQ // MIX_TM
    tok = lambda width: pl.BlockSpec((MIX_TM, width), lambda i: (i, 0))
    full = lambda shape: pl.BlockSpec(shape, lambda i: (0,) * len(shape))
    featT = pl.BlockSpec((D_MODEL, MIX_TM), lambda i: (0, i))
    return pl.pallas_call(
        _mix_kernel,
        grid=(n_steps,),
        in_specs=[
            tok(D_MODEL), tok(NA_WIDTH), tok(D_MODEL), tok(D_MODEL),
            full((NA_WIDTH, D_MODEL)), full((D_MODEL, D_MODEL)),
            full((D_MODEL, LANES)), full((D_MODEL, LANES)),
        ],
        out_specs=[featT, featT],
        out_shape=[
            jax.ShapeDtypeStruct((D_MODEL, SEQ), F32),
            jax.ShapeDtypeStruct((D_MODEL, SEQ), BF16),
        ],
        compiler_params=pltpu.CompilerParams(
            dimension_semantics=("arbitrary",), vmem_limit_bytes=VMEM_LIMIT),
        name="mix",
    )(x, y_na, gate_a, sg_part, wna_b, woutT_b, g1, b1)


def _cmpx(xs, i, j):
    hi = jnp.maximum(xs[i], xs[j])
    lo = jnp.minimum(xs[i], xs[j])
    xs[i], xs[j] = hi, lo


def _bitonic_sort_desc(xs):
    xs = list(xs)
    n = len(xs)
    size = 2
    while size <= n:
        stride = size // 2
        while stride >= 1:
            for i in range(n):
                l = i ^ stride
                if l > i:
                    if (i & size) == 0:
                        _cmpx(xs, i, l)
                    else:
                        _cmpx(xs, l, i)
            stride //= 2
        size *= 2
    return xs


def _bitonic_merge_desc(xs):
    xs = list(xs)
    n = len(xs)
    stride = n // 2
    while stride >= 1:
        for i in range(n):
            if (i & stride) == 0:
                _cmpx(xs, i, i + stride)
        stride //= 2
    return xs


def _top_merge(xs, ys):
    n = len(xs)
    return _bitonic_merge_desc([jnp.maximum(xs[i], ys[n - 1 - i]) for i in range(n)])


def _top16_of(vals):
    k = PEER_TOPK
    pad = (-len(vals)) % k
    vals = list(vals) + [jnp.full_like(vals[0], -jnp.inf)] * pad
    groups = [_bitonic_sort_desc(vals[g:g + k]) for g in range(0, len(vals), k)]
    top = groups[0]
    for grp in groups[1:]:
        top = _top_merge(top, grp)
    return top


def _route_kernel(x1Tb_ref, wqT_ref, k1_ref, k2_ref,
                  thr_ref, c1_ref, v2_ref, s2_ref, s1k_ref, s2k_ref):
    qpT = jnp.dot(wqT_ref[...], x1Tb_ref[...], preferred_element_type=F32).astype(BF16)
    for h in range(PEER_HEADS):
        lo = h * PEER_KEY_DIM
        s1 = jnp.dot(k1_ref[...], qpT[lo:lo + PEER_HALF, :], preferred_element_type=F32)
        s2 = jnp.dot(k2_ref[...], qpT[lo + PEER_HALF:lo + PEER_KEY_DIM, :],
                     preferred_element_type=F32)
        s2_ref[h] = s2
        for slab in range(ROUTE_TT // LANES):
            lanes = slice(slab * LANES, (slab + 1) * LANES)
            key_rows = pl.ds(h, PEER_N_KEYS, stride=PEER_HEADS)
            s1k_ref[slab, key_rows, :] = s1[:, lanes]
            s2k_ref[slab, key_rows, :] = s2[:, lanes]

    def route_slab(slab, carry):
        def key(ref, a):
            return ref[slab, SUBLANES * a:SUBLANES * (a + 1), :]

        v1 = _top16_of([key(s1k_ref, a) for a in range(PEER_N_KEYS)])
        v2 = _top16_of([key(s2k_ref, b) for b in range(PEER_N_KEYS)])
        cands = [v1[i] + v2[j] for i in range(PEER_TOPK) for j in range(PEER_TOPK)
                 if (i + 1) * (j + 1) <= PEER_TOPK]
        top = _top16_of(cands)
        tau = top[PEER_TOPK - 1]
        z = jnp.ones_like(tau)
        for t in top[1:]:
            z = z + jnp.exp(t - top[0])
        half_inv_z = 0.5 / z
        need = []
        for j in range(PEER_TOPK):
            v2_ref[slab, SUBLANES * j:SUBLANES * (j + 1), :] = v2[j]
            nj = jnp.full((SUBLANES, LANES), jnp.inf, F32)
            for i in range(PEER_TOPK):
                nj = jnp.where(v1[i] + v2[j] >= tau, v1[i], nj)
            need.append(nj)
        p2_top = [jnp.exp(v2[j] - v2[0]) for j in range(PEER_TOPK)]
        for a in range(PEER_N_KEYS):
            s1a = key(s1k_ref, a)
            thr = jnp.full((SUBLANES, LANES), PEER_NO_MATCH, F32)
            for j in range(PEER_TOPK):
                thr = jnp.where(s1a >= need[j], p2_top[j], thr)
            thr_ref[slab, SUBLANES * a:SUBLANES * (a + 1), :] = thr
            c1_ref[slab, SUBLANES * a:SUBLANES * (a + 1), :] = jnp.exp(s1a - v1[0]) * half_inv_z
        return carry

    lax.fori_loop(0, ROUTE_TT // LANES, route_slab, 0)


def _route(x1Tb, wqT_b, k1_b, k2_b):
    n_steps = SEQ // ROUTE_TT
    slabs = ROUTE_TT // LANES
    key_rows = PEER_N_KEYS * PEER_HEADS
    top_rows = PEER_TOPK * PEER_HEADS
    full = lambda shape: pl.BlockSpec(shape, lambda i: (0,) * len(shape))
    key_major = pl.BlockSpec((slabs, key_rows, LANES), lambda i: (i, 0, 0))
    key_major_shape = jax.ShapeDtypeStruct((SEQ // LANES, key_rows, LANES), F32)
    return pl.pallas_call(
        _route_kernel,
        grid=(n_steps,),
        in_specs=[
            pl.BlockSpec((D_MODEL, ROUTE_TT), lambda i: (0, i)),
            full((PEER_HEADS * PEER_KEY_DIM, D_MODEL)),
            full((PEER_N_KEYS, PEER_HALF)), full((PEER_N_KEYS, PEER_HALF)),
        ],
        out_specs=[
            key_major, key_major,
            pl.BlockSpec((slabs, top_rows, LANES), lambda i: (i, 0, 0)),
            pl.BlockSpec((PEER_HEADS, PEER_N_KEYS, ROUTE_TT), lambda i: (0, 0, i)),
        ],
        out_shape=[
            key_major_shape, key_major_shape,
            jax.ShapeDtypeStruct((SEQ // LANES, top_rows, LANES), F32),
            jax.ShapeDtypeStruct((PEER_HEADS, PEER_N_KEYS, SEQ), F32),
        ],
        scratch_shapes=[pltpu.VMEM((slabs, key_rows, LANES), F32),
                        pltpu.VMEM((slabs, key_rows, LANES), F32)],
        compiler_params=pltpu.CompilerParams(
            dimension_semantics=("arbitrary",), vmem_limit_bytes=VMEM_LIMIT),
        name="route",
    )(x1Tb, wqT_b, k1_b, k2_b)


def _peer_kernel(x1T_ref, x1Tb_ref, u_ref, vT_ref, s2_ref, v2_ref, thr_ref, c1_ref, g2_ref, b2_ref,
                 o_ref, acc_ref, p2_ref, a0_ref, a1_ref, w0_ref, w1_ref):
    e = pl.program_id(1)

    @pl.when(e == 0)
    def _():
        acc_ref[...] = jnp.zeros_like(acc_ref)
        for h in range(PEER_HEADS):
            for lt in range(PEER_TT // LANES):
                lanes = slice(lt * LANES, (lt + 1) * LANES)
                p2_ref[h, :, lanes] = jnp.exp(s2_ref[h, :, lanes] - v2_ref[lt, h:h + 1, :])

    blocks_per_tile = PEER_ET // PEER_N_KEYS
    n_groups = PEER_ET // PEER_GROUP
    blocks_per_group = PEER_GROUP // PEER_N_KEYS
    a_bufs = (a0_ref, a1_ref)
    w_bufs = (w0_ref, w1_ref)

    def scores(grp):
        rows = slice(grp * PEER_GROUP, (grp + 1) * PEER_GROUP)
        return jnp.dot(u_ref[rows, :], x1Tb_ref[...], preferred_element_type=F32)

    def contract(grp):
        cols = slice(grp * PEER_GROUP, (grp + 1) * PEER_GROUP)
        acc_ref[...] += jnp.dot(vT_ref[0, :, cols], w_bufs[grp % 2][...],
                                preferred_element_type=F32)

    def gate(grp):
        a_ref, w_ref = a_bufs[grp % 2], w_bufs[grp % 2]
        for blk in range(blocks_per_group):
            a = e * blocks_per_tile + grp * blocks_per_group + blk
            head_rows = pl.ds(pl.multiple_of(a * PEER_HEADS, PEER_HEADS), PEER_HEADS)
            for lt in range(PEER_TT // LANES):
                lanes = slice(lt * LANES, (lt + 1) * LANES)
                thr_heads = thr_ref[lt, head_rows, :]
                c1_heads = c1_ref[lt, head_rows, :]
                for rc in range(PEER_N_KEYS // PEER_CHUNK_ROWS):
                    rows = slice(rc * PEER_CHUNK_ROWS, (rc + 1) * PEER_CHUNK_ROWS)
                    g = None
                    for h in range(PEER_HEADS):
                        thr = thr_heads[h:h + 1, :]
                        c1 = c1_heads[h:h + 1, :]
                        p2 = p2_ref[h, rows, lanes]
                        term = jnp.where(p2 >= thr, p2 * c1, 0.0)
                        g = term if g is None else g + term
                    out_rows = slice(blk * PEER_N_KEYS + rc * PEER_CHUNK_ROWS,
                                     blk * PEER_N_KEYS + (rc + 1) * PEER_CHUNK_ROWS)
                    w_ref[out_rows, lanes] = _gelu_times(a_ref[out_rows, lanes], g)

    a_bufs[0][...] = scores(0)
    for grp in range(n_groups):
        if grp + 1 < n_groups:
            a_bufs[(grp + 1) % 2][...] = scores(grp + 1)
        gate(grp)
        if grp > 0:
            contract(grp - 1)
    contract(n_groups - 1)

    @pl.when(e == pl.num_programs(1) - 1)
    def _():
        z = DN_ALPHA * x1T_ref[...] + acc_ref[...]
        y = _layer_norm_cols(z, _tile_lanes(g2_ref[...], PEER_TT), _tile_lanes(b2_ref[...], PEER_TT))
        o_ref[...] = y.T


def _peer(x1T, x1Tb, u_b, vT_b, s2T, v2, thr, c1, g2, b2):
    n_tok = SEQ // PEER_TT
    n_exp = PEER_N_EXPERTS // PEER_ET
    featT = pl.BlockSpec((D_MODEL, PEER_TT), lambda j, e: (0, j))
    side = pl.BlockSpec((PEER_HEADS, PEER_N_KEYS, PEER_TT), lambda j, e: (0, 0, j))
    key_major = pl.BlockSpec((PEER_TT // LANES, PEER_N_KEYS * PEER_HEADS, LANES),
                             lambda j, e: (j, 0, 0))
    par = pl.BlockSpec((D_MODEL, LANES), lambda j, e: (0, 0))
    return pl.pallas_call(
        _peer_kernel,
        grid=(n_tok, n_exp),
        in_specs=[
            featT, featT,
            pl.BlockSpec((PEER_ET, D_MODEL), lambda j, e: (e, 0)),
            pl.BlockSpec((1, D_MODEL, PEER_ET), lambda j, e: (e, 0, 0)),
            side,
            pl.BlockSpec((PEER_TT // LANES, PEER_TOPK * PEER_HEADS, LANES), lambda j, e: (j, 0, 0)),
            key_major, key_major, par, par,
        ],
        out_specs=pl.BlockSpec((PEER_TT, D_MODEL), lambda j, e: (j, 0)),
        out_shape=jax.ShapeDtypeStruct((SEQ, D_MODEL), F32),
        scratch_shapes=[
            pltpu.VMEM((D_MODEL, PEER_TT), F32),
            pltpu.VMEM((PEER_HEADS, PEER_N_KEYS, PEER_TT), F32),
            pltpu.VMEM((PEER_GROUP, PEER_TT), F32),
            pltpu.VMEM((PEER_GROUP, PEER_TT), F32),
            pltpu.VMEM((PEER_GROUP, PEER_TT), BF16),
            pltpu.VMEM((PEER_GROUP, PEER_TT), BF16),
        ],
        compiler_params=pltpu.CompilerParams(
            dimension_semantics=("arbitrary", "arbitrary"), vmem_limit_bytes=VMEM_LIMIT),
        name="peer",
    )(x1T, x1Tb, u_b, vT_b, s2T, v2, thr, c1, g2, b2)


def _lane_replicated(p):
    return jnp.broadcast_to(p.astype(F32)[:, None], (p.shape[0], LANES))


def _transpose_cast_kernel(x_ref, o_ref):
    o_ref[0] = x_ref[...].T.astype(BF16)


def _transpose_cast(w, tile):
    rows, cols = w.shape
    per_tile = tile // XPOSE_ROWS
    return pl.pallas_call(
        _transpose_cast_kernel,
        grid=(rows // XPOSE_ROWS,),
        in_specs=[pl.BlockSpec((XPOSE_ROWS, cols), lambda i: (i, 0))],
        out_specs=pl.BlockSpec((1, cols, XPOSE_ROWS), lambda i: (i // per_tile, 0, i % per_tile)),
        out_shape=jax.ShapeDtypeStruct((rows // tile, cols, tile), BF16),
        compiler_params=pltpu.CompilerParams(
            dimension_semantics=("arbitrary",), vmem_limit_bytes=VMEM_LIMIT),
        name="transpose_cast",
    )(w)


def kernel(x, w_in, na_rpb, sg_norm_g, sg_norm_b, sg_spatial_w, sg_spatial_b, w_branch_na,
           w_branch_sg, w_out, ln1_g, ln1_b, peer_wq, peer_subkeys1, peer_subkeys2, peer_u, peer_v,
           ln2_g, ln2_b):
    assert x.shape == (1, SEQ, D_MODEL) and w_in.shape[0] == 1
    l = 0
    xs = x[0]
    q, k, v, gate_a, sg_part = _inproj(
        xs,
        w_in[l].astype(BF16),
        sg_norm_g[l][None, :], sg_norm_b[l][None, :],
        sg_spatial_w[l].astype(BF16),
        jnp.repeat(sg_spatial_b[l].T, SG_GROUP_DIM, axis=1),
        w_branch_sg[l].astype(BF16),
    )
    y_na = _natten(q, k, v, _natten_bias_table(na_rpb[l]))
    x1T, x1Tb = _mix(
        xs, y_na, gate_a, sg_part,
        w_branch_na[l].astype(BF16),
        _transpose_cast(w_out[l], D_MODEL)[0],
        _lane_replicated(ln1_g[l]), _lane_replicated(ln1_b[l]),
    )
    thr, c1, v2, s2T = _route(
        x1Tb, _transpose_cast(peer_wq[l], D_MODEL)[0],
        peer_subkeys1[l].astype(BF16), peer_subkeys2[l].astype(BF16),
    )
    out = _peer(
        x1T, x1Tb,
        peer_u[l].astype(BF16), _transpose_cast(peer_v[l], PEER_ET),
        s2T, v2, thr, c1,
        _lane_replicated(ln2_g[l]), _lane_replicated(ln2_b[l]),
    )
    return out[None]
```

```python
import math

import numpy as np
import jax
import jax.numpy as jnp
from jax import lax
from jax.experimental import pallas as pl
from jax.experimental.pallas import tpu as pltpu

F32 = jnp.float32
BF16 = jnp.bfloat16

D_MODEL = 1024
SEQ = 16384
GRID_W = 64
GRID_ROWS = SEQ // GRID_W
WIN_H = 8
WIN_W = 16
NA_HEADS = 8
NA_HEAD_DIM = 64
NA_WIDTH = NA_HEADS * NA_HEAD_DIM
SG_GROUPS = 4
SG_GROUP_DIM = 128
SG_WIDTH = SG_GROUPS * SG_GROUP_DIM
SG_CHUNK = 128
PEER_HEADS = 8
PEER_N_KEYS = 128
PEER_N_EXPERTS = PEER_N_KEYS * PEER_N_KEYS
PEER_KEY_DIM = 256
PEER_HALF = PEER_KEY_DIM // 2
PEER_TOPK = 16
DN_ALPHA = 2.0 ** 0.25
LN_EPS = 1e-5
MASK_BIAS = -1e30
PEER_NO_MATCH = 2.0
GELU_C0 = math.sqrt(2.0 / math.pi)
GELU_C1 = 0.044715 * GELU_C0

LANES = 128
SUBLANES = 8

INPROJ_TM = 512
NA_ROWS = 4
MIX_TM = 512
ROUTE_TT = 512
XPOSE_ROWS = 512
PEER_TT = 512
PEER_ET = 2048
PEER_GROUP = 512
PEER_CHUNK_ROWS = 64
VMEM_LIMIT = 56 * 1024 * 1024


def _gelu(x):
    return jax.nn.gelu(x)


def _gelu_times(x, g):
    u = x * (GELU_C0 + GELU_C1 * (x * x))
    return (x * (1.0 + jnp.tanh(u)) * g).astype(BF16)


def _layer_norm_rows(x, g, b):
    mu = jnp.mean(x, axis=-1, keepdims=True)
    xc = x - mu
    var = jnp.mean(xc * xc, axis=-1, keepdims=True)
    return xc * lax.rsqrt(var + LN_EPS) * g + b


def _layer_norm_cols(z, g, b):
    mu = jnp.mean(z, axis=0, keepdims=True)
    zc = z - mu
    var = jnp.mean(zc * zc, axis=0, keepdims=True)
    return zc * lax.rsqrt(var + LN_EPS) * g + b


def _tile_lanes(p128, width):
    reps = width // LANES
    return p128 if reps == 1 else jnp.concatenate([p128] * reps, axis=1)


def _inproj_kernel(x_ref, w_ref, ng_ref, nb_ref, ws_ref, bs_ref, wsg_ref,
                   q_ref, k_ref, v_ref, ga_ref, sg_ref):
    xb = x_ref[...].astype(BF16)

    def proj(lo, hi):
        return jnp.dot(xb, w_ref[:, lo:hi], preferred_element_type=F32)

    o = 0
    q_ref[...] = (proj(o, o + NA_WIDTH) * (NA_HEAD_DIM ** -0.5)).astype(BF16)
    o += NA_WIDTH
    k_ref[...] = proj(o, o + NA_WIDTH).astype(BF16)
    o += NA_WIDTH
    v_ref[...] = proj(o, o + NA_WIDTH).astype(BF16)
    o += NA_WIDTH
    u = _gelu(proj(o, o + SG_WIDTH))
    o += SG_WIDTH
    vn = _layer_norm_rows(_gelu(proj(o, o + SG_WIDTH)), ng_ref[...], nb_ref[...]).astype(BF16)
    o += SG_WIDTH
    ga_ref[...] = jax.nn.sigmoid(proj(o, o + D_MODEL))
    o += D_MODEL
    gate_b = jax.nn.sigmoid(proj(o, o + D_MODEL))

    chunks = []
    for c in range(INPROJ_TM // SG_CHUNK):
        vc = vn[c * SG_CHUNK:(c + 1) * SG_CHUNK, :]
        groups = [
            jnp.dot(ws_ref[g], vc[:, g * SG_GROUP_DIM:(g + 1) * SG_GROUP_DIM],
                    preferred_element_type=F32)
            for g in range(SG_GROUPS)
        ]
        chunks.append(jnp.concatenate(groups, axis=1) + bs_ref[...])
    y_sg = (u * jnp.concatenate(chunks, axis=0)).astype(BF16)
    sg_ref[...] = gate_b * jnp.dot(y_sg, wsg_ref[...], preferred_element_type=F32)


def _inproj(x, w_in_b, ng, nb, ws_b, bs_full, wsg_b):
    n_steps = SEQ // INPROJ_TM
    in_cols = w_in_b.shape[1]
    tok = lambda width: pl.BlockSpec((INPROJ_TM, width), lambda i: (i, 0))
    full = lambda shape: pl.BlockSpec(shape, lambda i: (0,) * len(shape))
    return pl.pallas_call(
        _inproj_kernel,
        grid=(n_steps,),
        in_specs=[
            tok(D_MODEL),
            full((D_MODEL, in_cols)),
            full((1, SG_WIDTH)),
            full((1, SG_WIDTH)),
            full((SG_GROUPS, SG_CHUNK, SG_CHUNK)),
            full((SG_CHUNK, SG_WIDTH)),
            full((SG_WIDTH, D_MODEL)),
        ],
        out_specs=[tok(NA_WIDTH), tok(NA_WIDTH), tok(NA_WIDTH), tok(D_MODEL), tok(D_MODEL)],
        out_shape=[
            jax.ShapeDtypeStruct((SEQ, NA_WIDTH), BF16),
            jax.ShapeDtypeStruct((SEQ, NA_WIDTH), BF16),
            jax.ShapeDtypeStruct((SEQ, NA_WIDTH), BF16),
            jax.ShapeDtypeStruct((SEQ, D_MODEL), F32),
            jax.ShapeDtypeStruct((SEQ, D_MODEL), F32),
        ],
        compiler_params=pltpu.CompilerParams(
            dimension_semantics=("arbitrary",), vmem_limit_bytes=VMEM_LIMIT),
        name="inproj",
    )(x, w_in_b, ng, nb, ws_b, bs_full, wsg_b)


def _natten_kernel(q_ref, kp_ref, kc_ref, kn_ref, vp_ref, vc_ref, vn_ref, bias_ref,
                   o_ref, kwin_ref, vwin_ref):
    i = pl.program_id(0)
    last = pl.num_programs(0) - 1
    kwin_ref[0:NA_ROWS] = kp_ref[...]
    kwin_ref[NA_ROWS:2 * NA_ROWS] = kc_ref[...]
    kwin_ref[2 * NA_ROWS:3 * NA_ROWS] = kn_ref[...]
    vwin_ref[0:NA_ROWS] = vp_ref[...]
    vwin_ref[NA_ROWS:2 * NA_ROWS] = vc_ref[...]
    vwin_ref[2 * NA_ROWS:3 * NA_ROWS] = vn_ref[...]

    n_keys = WIN_H * GRID_W
    lane_head = lax.broadcasted_iota(jnp.int32, (GRID_W, NA_WIDTH), 1) // NA_HEAD_DIM
    for j in range(NA_ROWS):
        off = jnp.where(i == 0, NA_ROWS, jnp.where(i == last, 0, j))
        dneg = jnp.where(i == 0, j, jnp.where(i == last, WIN_H // 2 + j, WIN_H // 2))
        kw = kwin_ref[pl.ds(off, WIN_H)].reshape(n_keys, NA_WIDTH)
        vw = vwin_ref[pl.ds(off, WIN_H)].reshape(n_keys, NA_WIDTH)
        qj = q_ref[j]
        qm = jnp.concatenate(
            [jnp.where(lane_head == h, qj, jnp.zeros_like(qj)) for h in range(NA_HEADS)], axis=0)
        logits = lax.dot_general(qm, kw, (((1,), (1,)), ((), ())), preferred_element_type=F32)
        logits = logits + bias_ref[dneg]
        m = jnp.max(logits, axis=-1, keepdims=True)
        p = jnp.exp(logits - m)
        inv = 1.0 / jnp.sum(p, axis=-1, keepdims=True)
        r = jnp.dot(p.astype(BF16), vw, preferred_element_type=F32) * inv
        acc = jnp.zeros((GRID_W, NA_WIDTH), F32)
        for h in range(NA_HEADS):
            acc = acc + jnp.where(lane_head == h, r[h * GRID_W:(h + 1) * GRID_W, :], 0.0)
        o_ref[j] = acc.astype(BF16)


def _natten(q, k, v, bias_tab):
    n_steps = GRID_ROWS // NA_ROWS
    q3 = q.reshape(GRID_ROWS, GRID_W, NA_WIDTH)
    k3 = k.reshape(GRID_ROWS, GRID_W, NA_WIDTH)
    v3 = v.reshape(GRID_ROWS, GRID_W, NA_WIDTH)
    blk = (NA_ROWS, GRID_W, NA_WIDTH)
    cur = pl.BlockSpec(blk, lambda i: (i, 0, 0))
    prev = pl.BlockSpec(blk, lambda i: (jnp.maximum(i - 1, 0), 0, 0))
    nxt = pl.BlockSpec(blk, lambda i: (jnp.minimum(i + 1, n_steps - 1), 0, 0))
    out = pl.pallas_call(
        _natten_kernel,
        grid=(n_steps,),
        in_specs=[cur, prev, cur, nxt, prev, cur, nxt,
                  pl.BlockSpec(bias_tab.shape, lambda i: (0, 0, 0))],
        out_specs=cur,
        out_shape=jax.ShapeDtypeStruct((GRID_ROWS, GRID_W, NA_WIDTH), BF16),
        scratch_shapes=[pltpu.VMEM((3 * NA_ROWS, GRID_W, NA_WIDTH), BF16),
                        pltpu.VMEM((3 * NA_ROWS, GRID_W, NA_WIDTH), BF16)],
        compiler_params=pltpu.CompilerParams(
            dimension_semantics=("arbitrary",), vmem_limit_bytes=VMEM_LIMIT),
        name="natten",
    )(q3, k3, k3, k3, v3, v3, v3, bias_tab)
    return out.reshape(SEQ, NA_WIDTH)


def _natten_bias_table(rpb):
    cols = np.arange(GRID_W)
    col_start = np.clip(cols - WIN_W // 2, 0, GRID_W - WIN_W)
    kc = np.arange(GRID_W)
    valid = (kc[None, :] >= col_start[:, None]) & (kc[None, :] < col_start[:, None] + WIN_W)
    dc = np.where(valid, kc[None, :] - cols[:, None] + (WIN_W - 1), -1)
    dr = np.arange(WIN_H)[None, :] + (WIN_H - 1) - np.arange(WIN_H)[:, None]
    rows = jnp.transpose(rpb[:, dr], (1, 0, 2, 3)).astype(F32)
    onehot = (dc[None, :, :] == np.arange(2 * WIN_W - 1)[:, None, None]).astype(np.float32)
    table = jnp.einsum('xhkd,dcz->xhckz', rows, jnp.asarray(onehot),
                       precision=lax.Precision.HIGHEST)
    outside = np.where(valid, 0.0, MASK_BIAS).astype(np.float32)
    table = table + jnp.asarray(outside)[None, None, :, None, :]
    return table.reshape(WIN_H, NA_HEADS * GRID_W, WIN_H * GRID_W)


def _mix_kernel(x_ref, yna_ref, ga_ref, sg_ref, wna_ref, woutT_ref, g1_ref, b1_ref,
                x1T_ref, x1Tb_ref):
    merged = ga_ref[...] * jnp.dot(yna_ref[...], wna_ref[...], preferred_element_type=F32)
    merged = (merged + sg_ref[...]).astype(BF16)
    outT = lax.dot_general(woutT_ref[...], merged, (((1,), (1,)), ((), ())),
                           preferred_element_type=F32)
    z = DN_ALPHA * x_ref[...].T + outT
    x1T = _layer_norm_cols(z, _tile_lanes(g1_ref[...], MIX_TM), _tile_lanes(b1_ref[...], MIX_TM))
    x1T_ref[...] = x1T
    x1Tb_ref[...] = x1T.astype(BF16)


def _mix(x, y_na, gate_a, sg_part, wna_b, woutT_b, g1, b1):
    n_steps = SEQ // MIX_TM
    tok = lambda width: pl.BlockSpec((MIX_TM, width), lambda i: (i, 0))
    full = lambda shape: pl.BlockSpec(shape, lambda i: (0,) * len(shape))
    featT = pl.BlockSpec((D_MODEL, MIX_TM), lambda i: (0, i))
    return pl.pallas_call(
        _mix_kernel,
        grid=(n_steps,),
        in_specs=[
            tok(D_MODEL), tok(NA_WIDTH), tok(D_MODEL), tok(D_MODEL),
            full((NA_WIDTH, D_MODEL)), full((D_MODEL, D_MODEL)),
            full((D_MODEL, LANES)), full((D_MODEL, LANES)),
        ],
        out_specs=[featT, featT],
        out_shape=[
            jax.ShapeDtypeStruct((D_MODEL, SEQ), F32),
            jax.ShapeDtypeStruct((D_MODEL, SEQ), BF16),
        ],
        compiler_params=pltpu.CompilerParams(
            dimension_semantics=("arbitrary",), vmem_limit_bytes=VMEM_LIMIT),
        name="mix",
    )(x, y_na, gate_a, sg_part, wna_b, woutT_b, g1, b1)


def _cmpx(xs, i, j):
    hi = jnp.maximum(xs[i], xs[j])
    lo = jnp.minimum(xs[i], xs[j])
    xs[i], xs[j] = hi, lo


def _bitonic_sort_desc(xs):
    xs = list(xs)
    n = len(xs)
    size = 2
    while size <= n:
        stride = size // 2
        while stride >= 1:
            for i in range(n):
                l = i ^ stride
                if l > i:
                    if (i & size) == 0:
                        _cmpx(xs, i, l)
                    else:
                        _cmpx(xs, l, i)
            stride //= 2
        size *= 2
    return xs


def _bitonic_merge_desc(xs):
    xs = list(xs)
    n = len(xs)
    stride = n // 2
    while stride >= 1:
        for i in range(n):
            if (i & stride) == 0:
                _cmpx(xs, i, i + stride)
        stride //= 2
    return xs


def _top_merge(xs, ys):
    n = len(xs)
    return _bitonic_merge_desc([jnp.maximum(xs[i], ys[n - 1 - i]) for i in range(n)])


def _top16_of(vals):
    k = PEER_TOPK
    pad = (-len(vals)) % k
    vals = list(vals) + [jnp.full_like(vals[0], -jnp.inf)] * pad
    groups = [_bitonic_sort_desc(vals[g:g + k]) for g in range(0, len(vals), k)]
    top = groups[0]
    for grp in groups[1:]:
        top = _top_merge(top, grp)
    return top


def _route_kernel(x1Tb_ref, wqT_ref, k1_ref, k2_ref,
                  thr_ref, c1_ref, v2_ref, s2_ref, s1k_ref, s2k_ref):
    qpT = jnp.dot(wqT_ref[...], x1Tb_ref[...], preferred_element_type=F32).astype(BF16)
    for h in range(PEER_HEADS):
        lo = h * PEER_KEY_DIM
        s1 = jnp.dot(k1_ref[...], qpT[lo:lo + PEER_HALF, :], preferred_element_type=F32)
        s2 = jnp.dot(k2_ref[...], qpT[lo + PEER_HALF:lo + PEER_KEY_DIM, :],
                     preferred_element_type=F32)
        s2_ref[h] = s2
        for slab in range(ROUTE_TT // LANES):
            lanes = slice(slab * LANES, (slab + 1) * LANES)
            key_rows = pl.ds(h, PEER_N_KEYS, stride=PEER_HEADS)
            s1k_ref[slab, key_rows, :] = s1[:, lanes]
            s2k_ref[slab, key_rows, :] = s2[:, lanes]

    def route_slab(slab, carry):
        def key(ref, a):
            return ref[slab, SUBLANES * a:SUBLANES * (a + 1), :]

        v1 = _top16_of([key(s1k_ref, a) for a in range(PEER_N_KEYS)])
        v2 = _top16_of([key(s2k_ref, b) for b in range(PEER_N_KEYS)])
        cands = [v1[i] + v2[j] for i in range(PEER_TOPK) for j in range(PEER_TOPK)
                 if (i + 1) * (j + 1) <= PEER_TOPK]
        top = _top16_of(cands)
        tau = top[PEER_TOPK - 1]
        z = jnp.ones_like(tau)
        for t in top[1:]:
            z = z + jnp.exp(t - top[0])
        half_inv_z = 0.5 / z
        need = []
        for j in range(PEER_TOPK):
            v2_ref[slab, SUBLANES * j:SUBLANES * (j + 1), :] = v2[j]
            nj = jnp.full((SUBLANES, LANES), jnp.inf, F32)
            for i in range(PEER_TOPK):
                nj = jnp.where(v1[i] + v2[j] >= tau, v1[i], nj)
            need.append(nj)
        p2_top = [jnp.exp(v2[j] - v2[0]) for j in range(PEER_TOPK)]
        for a in range(PEER_N_KEYS):
            s1a = key(s1k_ref, a)
            thr = jnp.full((SUBLANES, LANES), PEER_NO_MATCH, F32)
            for j in range(PEER_TOPK):
                thr = jnp.where(s1a >= need[j], p2_top[j], thr)
            thr_ref[slab, SUBLANES * a:SUBLANES * (a + 1), :] = thr
            c1_ref[slab, SUBLANES * a:SUBLANES * (a + 1), :] = jnp.exp(s1a - v1[0]) * half_inv_z
        return carry

    lax.fori_loop(0, ROUTE_TT // LANES, route_slab, 0)


def _route(x1Tb, wqT_b, k1_b, k2_b):
    n_steps = SEQ // ROUTE_TT
    slabs = ROUTE_TT // LANES
    key_rows = PEER_N_KEYS * PEER_HEADS
    top_rows = PEER_TOPK * PEER_HEADS
    full = lambda shape: pl.BlockSpec(shape, lambda i: (0,) * len(shape))
    key_major = pl.BlockSpec((slabs, key_rows, LANES), lambda i: (i, 0, 0))
    key_major_shape = jax.ShapeDtypeStruct((SEQ // LANES, key_rows, LANES), F32)
    return pl.pallas_call(
        _route_kernel,
        grid=(n_steps,),
        in_specs=[
            pl.BlockSpec((D_MODEL, ROUTE_TT), lambda i: (0, i)),
            full((PEER_HEADS * PEER_KEY_DIM, D_MODEL)),
            full((PEER_N_KEYS, PEER_HALF)), full((PEER_N_KEYS, PEER_HALF)),
        ],
        out_specs=[
            key_major, key_major,
            pl.BlockSpec((slabs, top_rows, LANES), lambda i: (i, 0, 0)),
            pl.BlockSpec((PEER_HEADS, PEER_N_KEYS, ROUTE_TT), lambda i: (0, 0, i)),
        ],
        out_shape=[
            key_major_shape, key_major_shape,
            jax.ShapeDtypeStruct((SEQ // LANES, top_rows, LANES), F32),
            jax.ShapeDtypeStruct((PEER_HEADS, PEER_N_KEYS, SEQ), F32),
        ],
        scratch_shapes=[pltpu.VMEM((slabs, key_rows, LANES), F32),
                        pltpu.VMEM((slabs, key_rows, LANES), F32)],
        compiler_params=pltpu.CompilerParams(
            dimension_semantics=("arbitrary",), vmem_limit_bytes=VMEM_LIMIT),
        name="route",
    )(x1Tb, wqT_b, k1_b, k2_b)


def _peer_kernel(x1T_ref, x1Tb_ref, u_ref, vT_ref, s2_ref, v2_ref, thr_ref, c1_ref, g2_ref, b2_ref,
                 o_ref, acc_ref, p2_ref, a0_ref, a1_ref, w0_ref, w1_ref):
    e = pl.program_id(1)

    @pl.when(e == 0)
    def _():
        acc_ref[...] = jnp.zeros_like(acc_ref)
        for h in range(PEER_HEADS):
            for lt in range(PEER_TT // LANES):
                lanes = slice(lt * LANES, (lt + 1) * LANES)
                p2_ref[h, :, lanes] = jnp.exp(s2_ref[h, :, lanes] - v2_ref[lt, h:h + 1, :])

    blocks_per_tile = PEER_ET // PEER_N_KEYS
    n_groups = PEER_ET // PEER_GROUP
    blocks_per_group = PEER_GROUP // PEER_N_KEYS
    a_bufs = (a0_ref, a1_ref)
    w_bufs = (w0_ref, w1_ref)

    def scores(grp):
        rows = slice(grp * PEER_GROUP, (grp + 1) * PEER_GROUP)
        return jnp.dot(u_ref[rows, :], x1Tb_ref[...], preferred_element_type=F32)

    def contract(grp):
        cols = slice(grp * PEER_GROUP, (grp + 1) * PEER_GROUP)
        acc_ref[...] += jnp.dot(vT_ref[0, :, cols], w_bufs[grp % 2][...],
                                preferred_element_type=F32)

    def gate(grp):
        a_ref, w_ref = a_bufs[grp % 2], w_bufs[grp % 2]
        for blk in range(blocks_per_group):
            a = e * blocks_per_tile + grp * blocks_per_group + blk
            head_rows = pl.ds(pl.multiple_of(a * PEER_HEADS, PEER_HEADS), PEER_HEADS)
            for lt in range(PEER_TT // LANES):
                lanes = slice(lt * LANES, (lt + 1) * LANES)
                thr_heads = thr_ref[lt, head_rows, :]
                c1_heads = c1_ref[lt, head_rows, :]
                for rc in range(PEER_N_KEYS // PEER_CHUNK_ROWS):
                    rows = slice(rc * PEER_CHUNK_ROWS, (rc + 1) * PEER_CHUNK_ROWS)
                    g = None
                    for h in range(PEER_HEADS):
                        thr = thr_heads[h:h + 1, :]
                        c1 = c1_heads[h:h + 1, :]
                        p2 = p2_ref[h, rows, lanes]
                        term = jnp.where(p2 >= thr, p2 * c1, 0.0)
                        g = term if g is None else g + term
                    out_rows = slice(blk * PEER_N_KEYS + rc * PEER_CHUNK_ROWS,
                                     blk * PEER_N_KEYS + (rc + 1) * PEER_CHUNK_ROWS)
                    w_ref[out_rows, lanes] = _gelu_times(a_ref[out_rows, lanes], g)

    a_bufs[0][...] = scores(0)
    for grp in range(n_groups):
        if grp + 1 < n_groups:
            a_bufs[(grp + 1) % 2][...] = scores(grp + 1)
        gate(grp)
        if grp > 0:
            contract(grp - 1)
    contract(n_groups - 1)

    @pl.when(e == pl.num_programs(1) - 1)
    def _():
        z = DN_ALPHA * x1T_ref[...] + acc_ref[...]
        y = _layer_norm_cols(z, _tile_lanes(g2_ref[...], PEER_TT), _tile_lanes(b2_ref[...], PEER_TT))
        o_ref[...] = y.T


def _peer(x1T, x1Tb, u_b, vT_b, s2T, v2, thr, c1, g2, b2):
    n_tok = SEQ // PEER_TT
    n_exp = PEER_N_EXPERTS // PEER_ET
    featT = pl.BlockSpec((D_MODEL, PEER_TT), lambda j, e: (0, j))
    side = pl.BlockSpec((PEER_HEADS, PEER_N_KEYS, PEER_TT), lambda j, e: (0, 0, j))
    key_major = pl.BlockSpec((PEER_TT // LANES, PEER_N_KEYS * PEER_HEADS, LANES),
                             lambda j, e: (j, 0, 0))
    par = pl.BlockSpec((D_MODEL, LANES), lambda j, e: (0, 0))
    return pl.pallas_call(
        _peer_kernel,
        grid=(n_tok, n_exp),
        in_specs=[
            featT, featT,
            pl.BlockSpec((PEER_ET, D_MODEL), lambda j, e: (e, 0)),
            pl.BlockSpec((1, D_MODEL, PEER_ET), lambda j, e: (e, 0, 0)),
            side,
            pl.BlockSpec((PEER_TT // LANES, PEER_TOPK * PEER_HEADS, LANES), lambda j, e: (j, 0, 0)),
            key_major, key_major, par, par,
        ],
        out_specs=pl.BlockSpec((PEER_TT, D_MODEL), lambda j, e: (j, 0)),
        out_shape=jax.ShapeDtypeStruct((SEQ, D_MODEL), F32),
        scratch_shapes=[
            pltpu.VMEM((D_MODEL, PEER_TT), F32),
            pltpu.VMEM((PEER_HEADS, PEER_N_KEYS, PEER_TT), F32),
            pltpu.VMEM((PEER_GROUP, PEER_TT), F32),
            pltpu.VMEM((PEER_GROUP, PEER_TT), F32),
            pltpu.VMEM((PEER_GROUP, PEER_TT), BF16),
            pltpu.VMEM((PEER_GROUP, PEER_TT), BF16),
        ],
        compiler_params=pltpu.CompilerParams(
            dimension_semantics=("arbitrary", "arbitrary"), vmem_limit_bytes=VMEM_LIMIT),
        name="peer",
    )(x1T, x1Tb, u_b, vT_b, s2T, v2, thr, c1, g2, b2)


def _lane_replicated(p):
    return jnp.broadcast_to(p.astype(F32)[:, None], (p.shape[0], LANES))


def _transpose_cast_kernel(x_ref, o_ref):
    o_ref[0] = x_ref[...].T.astype(BF16)


def _transpose_cast(w, tile):
    rows, cols = w.shape
    per_tile = tile // XPOSE_ROWS
    return pl.pallas_call(
        _transpose_cast_kernel,
        grid=(rows // XPOSE_ROWS,),
        in_specs=[pl.BlockSpec((XPOSE_ROWS, cols), lambda i: (i, 0))],
        out_specs=pl.BlockSpec((1, cols, XPOSE_ROWS), lambda i: (i // per_tile, 0, i % per_tile)),
        out_shape=jax.ShapeDtypeStruct((rows // tile, cols, tile), BF16),
        compiler_params=pltpu.CompilerParams(
            dimension_semantics=("arbitrary",), vmem_limit_bytes=VMEM_LIMIT),
        name="transpose_cast",
    )(w)


def kernel(x, w_in, na_rpb, sg_norm_g, sg_norm_b, sg_spatial_w, sg_spatial_b, w_branch_na,
           w_branch_sg, w_out, ln1_g, ln1_b, peer_wq, peer_subkeys1, peer_subkeys2, peer_u, peer_v,
           ln2_g, ln2_b):
    assert x.shape == (1, SEQ, D_MODEL) and w_in.shape[0] == 1
    l = 0
    xs = x[0]
    q, k, v, gate_a, sg_part = _inproj(
        xs,
        w_in[l].astype(BF16),
        sg_norm_g[l][None, :], sg_norm_b[l][None, :],
        sg_spatial_w[l].astype(BF16),
        jnp.repeat(sg_spatial_b[l].T, SG_GROUP_DIM, axis=1),
        w_branch_sg[l].astype(BF16),
    )
    y_na = _natten(q, k, v, _natten_bias_table(na_rpb[l]))
    x1T, x1Tb = _mix(
        xs, y_na, gate_a, sg_part,
        w_branch_na[l].astype(BF16),
        _transpose_cast(w_out[l], D_MODEL)[0],
        _lane_replicated(ln1_g[l]), _lane_replicated(ln1_b[l]),
    )
    thr, c1, v2, s2T = _route(
        x1Tb, _transpose_cast(peer_wq[l], D_MODEL)[0],
        peer_subkeys1[l].astype(BF16), peer_subkeys2[l].astype(BF16),
    )
    out = _peer(
        x1T, x1Tb,
        peer_u[l].astype(BF16), _transpose_cast(peer_v[l], PEER_ET),
        s2T, v2, thr, c1,
        _lane_replicated(ln2_g[l]), _lane_replicated(ln2_b[l]),
    )
    return out[None]
```

```python
import math

import numpy as np
import jax
import jax.numpy as jnp
from jax import lax
from jax.experimental import pallas as pl
from jax.experimental.pallas import tpu as pltpu

F32 = jnp.float32
BF16 = jnp.bfloat16

D_MODEL = 1024
SEQ = 16384
GRID_W = 64
GRID_ROWS = SEQ // GRID_W
WIN_H = 8
WIN_W = 16
NA_HEADS = 8
NA_HEAD_DIM = 64
NA_WIDTH = NA_HEADS * NA_HEAD_DIM
SG_GROUPS = 4
SG_GROUP_DIM = 128
SG_WIDTH = SG_GROUPS * SG_GROUP_DIM
SG_CHUNK = 128
PEER_HEADS = 8
PEER_N_KEYS = 128
PEER_N_EXPERTS = PEER_N_KEYS * PEER_N_KEYS
PEER_KEY_DIM = 256
PEER_HALF = PEER_KEY_DIM // 2
PEER_TOPK = 16
DN_ALPHA = 2.0 ** 0.25
LN_EPS = 1e-5
MASK_BIAS = -1e30
PEER_NO_MATCH = 2.0
GELU_C0 = math.sqrt(2.0 / math.pi)
GELU_C1 = 0.044715 * GELU_C0

LANES = 128
SUBLANES = 8

INPROJ_TM = 512
NA_ROWS = 4
NA_GROUP_WIDTH = 256
MIX_TM = 512
ROUTE_TT = 512
XPOSE_ROWS = 512
PEER_TT = 512
PEER_ET = 2048
PEER_GROUP = 512
PEER_CHUNK_ROWS = 64
VMEM_LIMIT = 56 * 1024 * 1024


def _gelu(x):
    return jax.nn.gelu(x)


def _gelu_times(x, g):
    u = x * (GELU_C0 + GELU_C1 * (x * x))
    return (x * (1.0 + jnp.tanh(u)) * g).astype(BF16)


def _layer_norm_rows(x, g, b):
    mu = jnp.mean(x, axis=-1, keepdims=True)
    xc = x - mu
    var = jnp.mean(xc * xc, axis=-1, keepdims=True)
    return xc * lax.rsqrt(var + LN_EPS) * g + b


def _layer_norm_cols(z, g, b):
    mu = jnp.mean(z, axis=0, keepdims=True)
    zc = z - mu
    var = jnp.mean(zc * zc, axis=0, keepdims=True)
    return zc * lax.rsqrt(var + LN_EPS) * g + b


def _tile_lanes(p128, width):
    reps = width // LANES
    return p128 if reps == 1 else jnp.concatenate([p128] * reps, axis=1)


def _inproj_kernel(x_ref, w_ref, ng_ref, nb_ref, ws_ref, bs_ref, wsg_ref,
                   q_ref, k_ref, v_ref, ga_ref, sg_ref):
    xb = x_ref[...].astype(BF16)

    def proj(lo, hi):
        return jnp.dot(xb, w_ref[:, lo:hi], preferred_element_type=F32)

    o = 0
    q_ref[...] = (proj(o, o + NA_WIDTH) * (NA_HEAD_DIM ** -0.5)).astype(BF16)
    o += NA_WIDTH
    k_ref[...] = proj(o, o + NA_WIDTH).astype(BF16)
    o += NA_WIDTH
    v_ref[...] = proj(o, o + NA_WIDTH).astype(BF16)
    o += NA_WIDTH
    u = _gelu(proj(o, o + SG_WIDTH))
    o += SG_WIDTH
    vn = _layer_norm_rows(_gelu(proj(o, o + SG_WIDTH)), ng_ref[...], nb_ref[...]).astype(BF16)
    o += SG_WIDTH
    ga_ref[...] = jax.nn.sigmoid(proj(o, o + D_MODEL))
    o += D_MODEL
    gate_b = jax.nn.sigmoid(proj(o, o + D_MODEL))

    chunks = []
    for c in range(INPROJ_TM // SG_CHUNK):
        vc = vn[c * SG_CHUNK:(c + 1) * SG_CHUNK, :]
        groups = [
            jnp.dot(ws_ref[g], vc[:, g * SG_GROUP_DIM:(g + 1) * SG_GROUP_DIM],
                    preferred_element_type=F32)
            for g in range(SG_GROUPS)
        ]
        chunks.append(jnp.concatenate(groups, axis=1) + bs_ref[...])
    y_sg = (u * jnp.concatenate(chunks, axis=0)).astype(BF16)
    sg_ref[...] = gate_b * jnp.dot(y_sg, wsg_ref[...], preferred_element_type=F32)


def _inproj(x, w_in_b, ng, nb, ws_b, bs_full, wsg_b):
    n_steps = SEQ // INPROJ_TM
    in_cols = w_in_b.shape[1]
    tok = lambda width: pl.BlockSpec((INPROJ_TM, width), lambda i: (i, 0))
    full = lambda shape: pl.BlockSpec(shape, lambda i: (0,) * len(shape))
    return pl.pallas_call(
        _inproj_kernel,
        grid=(n_steps,),
        in_specs=[
            tok(D_MODEL),
            full((D_MODEL, in_cols)),
            full((1, SG_WIDTH)),
            full((1, SG_WIDTH)),
            full((SG_GROUPS, SG_CHUNK, SG_CHUNK)),
            full((SG_CHUNK, SG_WIDTH)),
            full((SG_WIDTH, D_MODEL)),
        ],
        out_specs=[tok(NA_WIDTH), tok(NA_WIDTH), tok(NA_WIDTH), tok(D_MODEL), tok(D_MODEL)],
        out_shape=[
            jax.ShapeDtypeStruct((SEQ, NA_WIDTH), BF16),
            jax.ShapeDtypeStruct((SEQ, NA_WIDTH), BF16),
            jax.ShapeDtypeStruct((SEQ, NA_WIDTH), BF16),
            jax.ShapeDtypeStruct((SEQ, D_MODEL), F32),
            jax.ShapeDtypeStruct((SEQ, D_MODEL), F32),
        ],
        compiler_params=pltpu.CompilerParams(
            dimension_semantics=("arbitrary",), vmem_limit_bytes=VMEM_LIMIT),
        name="inproj",
    )(x, w_in_b, ng, nb, ws_b, bs_full, wsg_b)


def _natten_kernel(q_ref, kp_ref, kc_ref, kn_ref, vp_ref, vc_ref, vn_ref, bias_ref,
                   o_ref, kwin_ref, vwin_ref):
    i = pl.program_id(0)
    last = pl.num_programs(0) - 1
    kwin_ref[0:NA_ROWS] = kp_ref[...]
    kwin_ref[NA_ROWS:2 * NA_ROWS] = kc_ref[...]
    kwin_ref[2 * NA_ROWS:3 * NA_ROWS] = kn_ref[...]
    vwin_ref[0:NA_ROWS] = vp_ref[...]
    vwin_ref[NA_ROWS:2 * NA_ROWS] = vc_ref[...]
    vwin_ref[2 * NA_ROWS:3 * NA_ROWS] = vn_ref[...]

    n_keys = WIN_H * GRID_W
    group_heads = NA_GROUP_WIDTH // NA_HEAD_DIM
    n_groups = NA_WIDTH // NA_GROUP_WIDTH
    group_rows = group_heads * GRID_W
    lane_head = lax.broadcasted_iota(jnp.int32, (GRID_W, NA_GROUP_WIDTH), 1) // NA_HEAD_DIM
    for j in range(NA_ROWS):
        off = jnp.where(i == 0, NA_ROWS, jnp.where(i == last, 0, j))
        dneg = jnp.where(i == 0, j, jnp.where(i == last, WIN_H // 2 + j, WIN_H // 2))
        kw = kwin_ref[pl.ds(off, WIN_H)].reshape(n_keys, NA_WIDTH)
        vw = vwin_ref[pl.ds(off, WIN_H)].reshape(n_keys, NA_WIDTH)
        qj = q_ref[j]
        outs = []
        for g in range(n_groups):
            chans = slice(g * NA_GROUP_WIDTH, (g + 1) * NA_GROUP_WIDTH)
            qg = qj[:, chans]
            qm = jnp.concatenate(
                [jnp.where(lane_head == h, qg, jnp.zeros_like(qg)) for h in range(group_heads)],
                axis=0)
            logits = lax.dot_general(qm, kw[:, chans], (((1,), (1,)), ((), ())),
                                     preferred_element_type=F32)
            logits = logits + bias_ref[dneg, g * group_rows:(g + 1) * group_rows, :]
            m = jnp.max(logits, axis=-1, keepdims=True)
            p = jnp.exp(logits - m)
            inv = 1.0 / jnp.sum(p, axis=-1, keepdims=True)
            r = jnp.dot(p.astype(BF16), vw[:, chans], preferred_element_type=F32) * inv
            acc = jnp.zeros((GRID_W, NA_GROUP_WIDTH), F32)
            for h in range(group_heads):
                acc = acc + jnp.where(lane_head == h, r[h * GRID_W:(h + 1) * GRID_W, :], 0.0)
            outs.append(acc.astype(BF16))
        o_ref[j] = jnp.concatenate(outs, axis=1)


def _natten(q, k, v, bias_tab):
    n_steps = GRID_ROWS // NA_ROWS
    q3 = q.reshape(GRID_ROWS, GRID_W, NA_WIDTH)
    k3 = k.reshape(GRID_ROWS, GRID_W, NA_WIDTH)
    v3 = v.reshape(GRID_ROWS, GRID_W, NA_WIDTH)
    blk = (NA_ROWS, GRID_W, NA_WIDTH)
    cur = pl.BlockSpec(blk, lambda i: (i, 0, 0))
    prev = pl.BlockSpec(blk, lambda i: (jnp.maximum(i - 1, 0), 0, 0))
    nxt = pl.BlockSpec(blk, lambda i: (jnp.minimum(i + 1, n_steps - 1), 0, 0))
    out = pl.pallas_call(
        _natten_kernel,
        grid=(n_steps,),
        in_specs=[cur, prev, cur, nxt, prev, cur, nxt,
                  pl.BlockSpec(bias_tab.shape, lambda i: (0, 0, 0))],
        out_specs=cur,
        out_shape=jax.ShapeDtypeStruct((GRID_ROWS, GRID_W, NA_WIDTH), BF16),
        scratch_shapes=[pltpu.VMEM((3 * NA_ROWS, GRID_W, NA_WIDTH), BF16),
                        pltpu.VMEM((3 * NA_ROWS, GRID_W, NA_WIDTH), BF16)],
        compiler_params=pltpu.CompilerParams(
            dimension_semantics=("arbitrary",), vmem_limit_bytes=VMEM_LIMIT),
        name="natten",
    )(q3, k3, k3, k3, v3, v3, v3, bias_tab)
    return out.reshape(SEQ, NA_WIDTH)


def _natten_bias_table(rpb):
    cols = np.arange(GRID_W)
    col_start = np.clip(cols - WIN_W // 2, 0, GRID_W - WIN_W)
    kc = np.arange(GRID_W)
    valid = (kc[None, :] >= col_start[:, None]) & (kc[None, :] < col_start[:, None] + WIN_W)
    dc = np.where(valid, kc[None, :] - cols[:, None] + (WIN_W - 1), -1)
    dr = np.arange(WIN_H)[None, :] + (WIN_H - 1) - np.arange(WIN_H)[:, None]
    rows = jnp.transpose(rpb[:, dr], (1, 0, 2, 3)).astype(F32)
    onehot = (dc[None, :, :] == np.arange(2 * WIN_W - 1)[:, None, None]).astype(np.float32)
    table = jnp.einsum('xhkd,dcz->xhckz', rows, jnp.asarray(onehot),
                       precision=lax.Precision.HIGHEST)
    outside = np.where(valid, 0.0, MASK_BIAS).astype(np.float32)
    table = table + jnp.asarray(outside)[None, None, :, None, :]
    return table.reshape(WIN_H, NA_HEADS * GRID_W, WIN_H * GRID_W)


def _mix_kernel(x_ref, yna_ref, ga_ref, sg_ref, wna_ref, woutT_ref, g1_ref, b1_ref,
                x1T_ref, x1Tb_ref):
    merged = ga_ref[...] * jnp.dot(yna_ref[...], wna_ref[...], preferred_element_type=F32)
    merged = (merged + sg_ref[...]).astype(BF16)
    outT = lax.dot_general(woutT_ref[...], merged, (((1,), (1,)), ((), ())),
                           preferred_element_type=F32)
    z = DN_ALPHA * x_ref[...].T + outT
    x1T = _layer_norm_cols(z, _tile_lanes(g1_ref[...], MIX_TM), _tile_lanes(b1_ref[...], MIX_TM))
    x1T_ref[...] = x1T
    x1Tb_ref[...] = x1T.astype(BF16)


def _mix(x, y_na, gate_a, sg_part, wna_b, woutT_b, g1, b1):
    n_steps = SEQ // MIX_TM
    tok = lambda width: pl.BlockSpec((MIX_TM, width), lambda i: (i, 0))
    full = lambda shape: pl.BlockSpec(shape, lambda i: (0,) * len(shape))
    featT = pl.BlockSpec((D_MODEL, MIX_TM), lambda i: (0, i))
    return pl.pallas_call(
        _mix_kernel,
        grid=(n_steps,),
        in_specs=[
            tok(D_MODEL), tok(NA_WIDTH), tok(D_MODEL), tok(D_MODEL),
            full((NA_WIDTH, D_MODEL)), full((D_MODEL, D_MODEL)),
            full((D_MODEL, LANES)), full((D_MODEL, LANES)),
        ],
        out_specs=[featT, featT],
        out_shape=[
            jax.ShapeDtypeStruct((D_MODEL, SEQ), F32),
            jax.ShapeDtypeStruct((D_MODEL, SEQ), BF16),
        ],
        compiler_params=pltpu.CompilerParams(
            dimension_semantics=("arbitrary",), vmem_limit_bytes=VMEM_LIMIT),
        name="mix",
    )(x, y_na, gate_a, sg_part, wna_b, woutT_b, g1, b1)


def _cmpx(xs, i, j):
    hi = jnp.maximum(xs[i], xs[j])
    lo = jnp.minimum(xs[i], xs[j])
    xs[i], xs[j] = hi, lo


def _bitonic_sort_desc(xs):
    xs = list(xs)
    n = len(xs)
    size = 2
    while size <= n:
        stride = size // 2
        while stride >= 1:
            for i in range(n):
                l = i ^ stride
                if l > i:
                    if (i & size) == 0:
                        _cmpx(xs, i, l)
                    else:
                        _cmpx(xs, l, i)
            stride //= 2
        size *= 2
    return xs


def _bitonic_merge_desc(xs):
    xs = list(xs)
    n = len(xs)
    stride = n // 2
    while stride >= 1:
        for i in range(n):
            if (i & stride) == 0:
                _cmpx(xs, i, i + stride)
        stride //= 2
    return xs


def _top_merge(xs, ys):
    n = len(xs)
    return _bitonic_merge_desc([jnp.maximum(xs[i], ys[n - 1 - i]) for i in range(n)])


def _top16_of(vals):
    k = PEER_TOPK
    pad = (-len(vals)) % k
    vals = list(vals) + [jnp.full_like(vals[0], -jnp.inf)] * pad
    groups = [_bitonic_sort_desc(vals[g:g + k]) for g in range(0, len(vals), k)]
    top = groups[0]
    for grp in groups[1:]:
        top = _top_merge(top, grp)
    return top


def _route_kernel(x1Tb_ref, wqT_ref, k1_ref, k2_ref,
                  thr_ref, c1_ref, v2_ref, s2_ref, s1k_ref, s2k_ref):
    qpT = jnp.dot(wqT_ref[...], x1Tb_ref[...], preferred_element_type=F32).astype(BF16)
    for h in range(PEER_HEADS):
        lo = h * PEER_KEY_DIM
        s1 = jnp.dot(k1_ref[...], qpT[lo:lo + PEER_HALF, :], preferred_element_type=F32)
        s2 = jnp.dot(k2_ref[...], qpT[lo + PEER_HALF:lo + PEER_KEY_DIM, :],
                     preferred_element_type=F32)
        s2_ref[h] = s2
        for slab in range(ROUTE_TT // LANES):
            lanes = slice(slab * LANES, (slab + 1) * LANES)
            key_rows = pl.ds(h, PEER_N_KEYS, stride=PEER_HEADS)
            s1k_ref[slab, key_rows, :] = s1[:, lanes]
            s2k_ref[slab, key_rows, :] = s2[:, lanes]

    def route_slab(slab, carry):
        def key(ref, a):
            return ref[slab, SUBLANES * a:SUBLANES * (a + 1), :]

        v1 = _top16_of([key(s1k_ref, a) for a in range(PEER_N_KEYS)])
        v2 = _top16_of([key(s2k_ref, b) for b in range(PEER_N_KEYS)])
        cands = [v1[i] + v2[j] for i in range(PEER_TOPK) for j in range(PEER_TOPK)
                 if (i + 1) * (j + 1) <= PEER_TOPK]
        top = _top16_of(cands)
        tau = top[PEER_TOPK - 1]
        z = jnp.ones_like(tau)
        for t in top[1:]:
            z = z + jnp.exp(t - top[0])
        half_inv_z = 0.5 / z
        need = []
        for j in range(PEER_TOPK):
            v2_ref[slab, SUBLANES * j:SUBLANES * (j + 1), :] = v2[j]
            nj = jnp.full((SUBLANES, LANES), jnp.inf, F32)
            for i in range(PEER_TOPK):
                nj = jnp.where(v1[i] + v2[j] >= tau, v1[i], nj)
            need.append(nj)
        p2_top = [jnp.exp(v2[j] - v2[0]) for j in range(PEER_TOPK)]
        for a in range(PEER_N_KEYS):
            s1a = key(s1k_ref, a)
            thr = jnp.full((SUBLANES, LANES), PEER_NO_MATCH, F32)
            for j in range(PEER_TOPK):
                thr = jnp.where(s1a >= need[j], p2_top[j], thr)
            thr_ref[slab, SUBLANES * a:SUBLANES * (a + 1), :] = thr
            c1_ref[slab, SUBLANES * a:SUBLANES * (a + 1), :] = jnp.exp(s1a - v1[0]) * half_inv_z
        return carry

    lax.fori_loop(0, ROUTE_TT // LANES, route_slab, 0)


def _route(x1Tb, wqT_b, k1_b, k2_b):
    n_steps = SEQ // ROUTE_TT
    slabs = ROUTE_TT // LANES
    key_rows = PEER_N_KEYS * PEER_HEADS
    top_rows = PEER_TOPK * PEER_HEADS
    full = lambda shape: pl.BlockSpec(shape, lambda i: (0,) * len(shape))
    key_major = pl.BlockSpec((slabs, key_rows, LANES), lambda i: (i, 0, 0))
    key_major_shape = jax.ShapeDtypeStruct((SEQ // LANES, key_rows, LANES), F32)
    return pl.pallas_call(
        _route_kernel,
        grid=(n_steps,),
        in_specs=[
            pl.BlockSpec((D_MODEL, ROUTE_TT), lambda i: (0, i)),
            full((PEER_HEADS * PEER_KEY_DIM, D_MODEL)),
            full((PEER_N_KEYS, PEER_HALF)), full((PEER_N_KEYS, PEER_HALF)),
        ],
        out_specs=[
            key_major, key_major,
            pl.BlockSpec((slabs, top_rows, LANES), lambda i: (i, 0, 0)),
            pl.BlockSpec((PEER_HEADS, PEER_N_KEYS, ROUTE_TT), lambda i: (0, 0, i)),
        ],
        out_shape=[
            key_major_shape, key_major_shape,
            jax.ShapeDtypeStruct((SEQ // LANES, top_rows, LANES), F32),
            jax.ShapeDtypeStruct((PEER_HEADS, PEER_N_KEYS, SEQ), F32),
        ],
        scratch_shapes=[pltpu.VMEM((slabs, key_rows, LANES), F32),
                        pltpu.VMEM((slabs, key_rows, LANES), F32)],
        compiler_params=pltpu.CompilerParams(
            dimension_semantics=("arbitrary",), vmem_limit_bytes=VMEM_LIMIT),
        name="route",
    )(x1Tb, wqT_b, k1_b, k2_b)


def _peer_kernel(x1T_ref, x1Tb_ref, u_ref, vT_ref, s2_ref, v2_ref, thr_ref, c1_ref, g2_ref, b2_ref,
                 o_ref, acc_ref, p2_ref, a0_ref, a1_ref, w0_ref, w1_ref):
    e = pl.program_id(1)

    @pl.when(e == 0)
    def _():
        acc_ref[...] = jnp.zeros_like(acc_ref)
        for h in range(PEER_HEADS):
            for lt in range(PEER_TT // LANES):
                lanes = slice(lt * LANES, (lt + 1) * LANES)
                p2_ref[h, :, lanes] = jnp.exp(s2_ref[h, :, lanes] - v2_ref[lt, h:h + 1, :])

    blocks_per_tile = PEER_ET // PEER_N_KEYS
    n_groups = PEER_ET // PEER_GROUP
    blocks_per_group = PEER_GROUP // PEER_N_KEYS
    a_bufs = (a0_ref, a1_ref)
    w_bufs = (w0_ref, w1_ref)

    def scores(grp):
        rows = slice(grp * PEER_GROUP, (grp + 1) * PEER_GROUP)
        return jnp.dot(u_ref[rows, :], x1Tb_ref[...], preferred_element_type=F32)

    def contract(grp):
        cols = slice(grp * PEER_GROUP, (grp + 1) * PEER_GROUP)
        acc_ref[...] += jnp.dot(vT_ref[0, :, cols], w_bufs[grp % 2][...],
                                preferred_element_type=F32)

    def gate(grp):
        a_ref, w_ref = a_bufs[grp % 2], w_bufs[grp % 2]
        for blk in range(blocks_per_group):
            a = e * blocks_per_tile + grp * blocks_per_group + blk
            head_rows = pl.ds(pl.multiple_of(a * PEER_HEADS, PEER_HEADS), PEER_HEADS)
            for lt in range(PEER_TT // LANES):
                lanes = slice(lt * LANES, (lt + 1) * LANES)
                thr_heads = thr_ref[lt, head_rows, :]
                c1_heads = c1_ref[lt, head_rows, :]
                for rc in range(PEER_N_KEYS // PEER_CHUNK_ROWS):
                    rows = slice(rc * PEER_CHUNK_ROWS, (rc + 1) * PEER_CHUNK_ROWS)
                    g = None
                    for h in range(PEER_HEADS):
                        thr = thr_heads[h:h + 1, :]
                        c1 = c1_heads[h:h + 1, :]
                        p2 = p2_ref[h, rows, lanes]
                        term = jnp.where(p2 >= thr, p2 * c1, 0.0)
                        g = term if g is None else g + term
                    out_rows = slice(blk * PEER_N_KEYS + rc * PEER_CHUNK_ROWS,
                                     blk * PEER_N_KEYS + (rc + 1) * PEER_CHUNK_ROWS)
                    w_ref[out_rows, lanes] = _gelu_times(a_ref[out_rows, lanes], g)

    a_bufs[0][...] = scores(0)
    for grp in range(n_groups):
        if grp + 1 < n_groups:
            a_bufs[(grp + 1) % 2][...] = scores(grp + 1)
        gate(grp)
        if grp > 0:
            contract(grp - 1)
    contract(n_groups - 1)

    @pl.when(e == pl.num_programs(1) - 1)
    def _():
        z = DN_ALPHA * x1T_ref[...] + acc_ref[...]
        y = _layer_norm_cols(z, _tile_lanes(g2_ref[...], PEER_TT), _tile_lanes(b2_ref[...], PEER_TT))
        o_ref[...] = y.T


def _peer(x1T, x1Tb, u_b, vT_b, s2T, v2, thr, c1, g2, b2):
    n_tok = SEQ // PEER_TT
    n_exp = PEER_N_EXPERTS // PEER_ET
    featT = pl.BlockSpec((D_MODEL, PEER_TT), lambda j, e: (0, j))
    side = pl.BlockSpec((PEER_HEADS, PEER_N_KEYS, PEER_TT), lambda j, e: (0, 0, j))
    key_major = pl.BlockSpec((PEER_TT // LANES, PEER_N_KEYS * PEER_HEADS, LANES),
                             lambda j, e: (j, 0, 0))
    par = pl.BlockSpec((D_MODEL, LANES), lambda j, e: (0, 0))
    return pl.pallas_call(
        _peer_kernel,
        grid=(n_tok, n_exp),
        in_specs=[
            featT, featT,
            pl.BlockSpec((PEER_ET, D_MODEL), lambda j, e: (e, 0)),
            pl.BlockSpec((1, D_MODEL, PEER_ET), lambda j, e: (e, 0, 0)),
            side,
            pl.BlockSpec((PEER_TT // LANES, PEER_TOPK * PEER_HEADS, LANES), lambda j, e: (j, 0, 0)),
            key_major, key_major, par, par,
        ],
        out_specs=pl.BlockSpec((PEER_TT, D_MODEL), lambda j, e: (j, 0)),
        out_shape=jax.ShapeDtypeStruct((SEQ, D_MODEL), F32),
        scratch_shapes=[
            pltpu.VMEM((D_MODEL, PEER_TT), F32),
            pltpu.VMEM((PEER_HEADS, PEER_N_KEYS, PEER_TT), F32),
            pltpu.VMEM((PEER_GROUP, PEER_TT), F32),
            pltpu.VMEM((PEER_GROUP, PEER_TT), F32),
            pltpu.VMEM((PEER_GROUP, PEER_TT), BF16),
            pltpu.VMEM((PEER_GROUP, PEER_TT), BF16),
        ],
        compiler_params=pltpu.CompilerParams(
            dimension_semantics=("arbitrary", "arbitrary"), vmem_limit_bytes=VMEM_LIMIT),
        name="peer",
    )(x1T, x1Tb, u_b, vT_b, s2T, v2, thr, c1, g2, b2)


def _lane_replicated(p):
    return jnp.broadcast_to(p.astype(F32)[:, None], (p.shape[0], LANES))


def _transpose_cast_kernel(x_ref, o_ref):
    o_ref[0] = x_ref[...].T.astype(BF16)


def _transpose_cast(w, tile):
    rows, cols = w.shape
    per_tile = tile // XPOSE_ROWS
    return pl.pallas_call(
        _transpose_cast_kernel,
        grid=(rows // XPOSE_ROWS,),
        in_specs=[pl.BlockSpec((XPOSE_ROWS, cols), lambda i: (i, 0))],
        out_specs=pl.BlockSpec((1, cols, XPOSE_ROWS), lambda i: (i // per_tile, 0, i % per_tile)),
        out_shape=jax.ShapeDtypeStruct((rows // tile, cols, tile), BF16),
        compiler_params=pltpu.CompilerParams(
            dimension_semantics=("arbitrary",), vmem_limit_bytes=VMEM_LIMIT),
        name="transpose_cast",
    )(w)


def kernel(x, w_in, na_rpb, sg_norm_g, sg_norm_b, sg_spatial_w, sg_spatial_b, w_branch_na,
           w_branch_sg, w_out, ln1_g, ln1_b, peer_wq, peer_subkeys1, peer_subkeys2, peer_u, peer_v,
           ln2_g, ln2_b):
    assert x.shape == (1, SEQ, D_MODEL) and w_in.shape[0] == 1
    l = 0
    xs = x[0]
    q, k, v, gate_a, sg_part = _inproj(
        xs,
        w_in[l].astype(BF16),
        sg_norm_g[l][None, :], sg_norm_b[l][None, :],
        sg_spatial_w[l].astype(BF16),
        jnp.repeat(sg_spatial_b[l].T, SG_GROUP_DIM, axis=1),
        w_branch_sg[l].astype(BF16),
    )
    y_na = _natten(q, k, v, _natten_bias_table(na_rpb[l]))
    x1T, x1Tb = _mix(
        xs, y_na, gate_a, sg_part,
        w_branch_na[l].astype(BF16),
        _transpose_cast(w_out[l], D_MODEL)[0],
        _lane_replicated(ln1_g[l]), _lane_replicated(ln1_b[l]),
    )
    thr, c1, v2, s2T = _route(
        x1Tb, _transpose_cast(peer_wq[l], D_MODEL)[0],
        peer_subkeys1[l].astype(BF16), peer_subkeys2[l].astype(BF16),
    )
    out = _peer(
        x1T, x1Tb,
        peer_u[l].astype(BF16), _transpose_cast(peer_v[l], PEER_ET),
        s2T, v2, thr, c1,
        _lane_replicated(ln2_g[l]), _lane_replicated(ln2_b[l]),
    )
    return out[None]
```

```python
import math

import numpy as np
import jax
import jax.numpy as jnp
from jax import lax
from jax.experimental import pallas as pl
from jax.experimental.pallas import tpu as pltpu

F32 = jnp.float32
BF16 = jnp.bfloat16

D_MODEL = 1024
SEQ = 16384
GRID_W = 64
GRID_ROWS = SEQ // GRID_W
WIN_H = 8
WIN_W = 16
NA_HEADS = 8
NA_HEAD_DIM = 64
NA_WIDTH = NA_HEADS * NA_HEAD_DIM
SG_GROUPS = 4
SG_GROUP_DIM = 128
SG_WIDTH = SG_GROUPS * SG_GROUP_DIM
SG_CHUNK = 128
PEER_HEADS = 8
PEER_N_KEYS = 128
PEER_N_EXPERTS = PEER_N_KEYS * PEER_N_KEYS
PEER_KEY_DIM = 256
PEER_HALF = PEER_KEY_DIM // 2
PEER_TOPK = 16
DN_ALPHA = 2.0 ** 0.25
LN_EPS = 1e-5
MASK_BIAS = -1e30
PEER_NO_MATCH = 2.0
GELU_C0 = math.sqrt(2.0 / math.pi)
GELU_C1 = 0.044715 * GELU_C0

LANES = 128
SUBLANES = 8

INPROJ_TM = 512
NA_ROWS = 8
NA_GROUP_WIDTH = 256
MIX_TM = 512
ROUTE_TT = 512
XPOSE_ROWS = 512
PEER_TT = 512
PEER_ET = 2048
PEER_GROUP = 1024
PEER_CHUNK_ROWS = 64
VMEM_LIMIT = 56 * 1024 * 1024


def _gelu(x):
    return jax.nn.gelu(x)


def _gelu_times(x, g):
    u = x * (GELU_C0 + GELU_C1 * (x * x))
    return (x * (1.0 + jnp.tanh(u)) * g).astype(BF16)


def _layer_norm_rows(x, g, b):
    mu = jnp.mean(x, axis=-1, keepdims=True)
    xc = x - mu
    var = jnp.mean(xc * xc, axis=-1, keepdims=True)
    return xc * lax.rsqrt(var + LN_EPS) * g + b


def _layer_norm_cols(z, g, b):
    mu = jnp.mean(z, axis=0, keepdims=True)
    zc = z - mu
    var = jnp.mean(zc * zc, axis=0, keepdims=True)
    return zc * lax.rsqrt(var + LN_EPS) * g + b


def _tile_lanes(p128, width):
    reps = width // LANES
    return p128 if reps == 1 else jnp.concatenate([p128] * reps, axis=1)


def _inproj_kernel(x_ref, w_ref, ng_ref, nb_ref, ws_ref, bs_ref, wsg_ref,
                   q_ref, k_ref, v_ref, ga_ref, sg_ref):
    xb = x_ref[...].astype(BF16)

    def proj(lo, hi):
        return jnp.dot(xb, w_ref[:, lo:hi], preferred_element_type=F32)

    o = 0
    q_ref[...] = (proj(o, o + NA_WIDTH) * (NA_HEAD_DIM ** -0.5)).astype(BF16)
    o += NA_WIDTH
    k_ref[...] = proj(o, o + NA_WIDTH).astype(BF16)
    o += NA_WIDTH
    v_ref[...] = proj(o, o + NA_WIDTH).astype(BF16)
    o += NA_WIDTH
    u = _gelu(proj(o, o + SG_WIDTH))
    o += SG_WIDTH
    vn = _layer_norm_rows(_gelu(proj(o, o + SG_WIDTH)), ng_ref[...], nb_ref[...]).astype(BF16)
    o += SG_WIDTH
    ga_ref[...] = jax.nn.sigmoid(proj(o, o + D_MODEL))
    o += D_MODEL
    gate_b = jax.nn.sigmoid(proj(o, o + D_MODEL))

    chunks = []
    for c in range(INPROJ_TM // SG_CHUNK):
        vc = vn[c * SG_CHUNK:(c + 1) * SG_CHUNK, :]
        groups = [
            jnp.dot(ws_ref[g], vc[:, g * SG_GROUP_DIM:(g + 1) * SG_GROUP_DIM],
                    preferred_element_type=F32)
            for g in range(SG_GROUPS)
        ]
        chunks.append(jnp.concatenate(groups, axis=1) + bs_ref[...])
    y_sg = (u * jnp.concatenate(chunks, axis=0)).astype(BF16)
    sg_ref[...] = gate_b * jnp.dot(y_sg, wsg_ref[...], preferred_element_type=F32)


def _inproj(x, w_in_b, ng, nb, ws_b, bs_full, wsg_b):
    n_steps = SEQ // INPROJ_TM
    in_cols = w_in_b.shape[1]
    tok = lambda width: pl.BlockSpec((INPROJ_TM, width), lambda i: (i, 0))
    full = lambda shape: pl.BlockSpec(shape, lambda i: (0,) * len(shape))
    return pl.pallas_call(
        _inproj_kernel,
        grid=(n_steps,),
        in_specs=[
            tok(D_MODEL),
            full((D_MODEL, in_cols)),
            full((1, SG_WIDTH)),
            full((1, SG_WIDTH)),
            full((SG_GROUPS, SG_CHUNK, SG_CHUNK)),
            full((SG_CHUNK, SG_WIDTH)),
            full((SG_WIDTH, D_MODEL)),
        ],
        out_specs=[tok(NA_WIDTH), tok(NA_WIDTH), tok(NA_WIDTH), tok(D_MODEL), tok(D_MODEL)],
        out_shape=[
            jax.ShapeDtypeStruct((SEQ, NA_WIDTH), BF16),
            jax.ShapeDtypeStruct((SEQ, NA_WIDTH), BF16),
            jax.ShapeDtypeStruct((SEQ, NA_WIDTH), BF16),
            jax.ShapeDtypeStruct((SEQ, D_MODEL), F32),
            jax.ShapeDtypeStruct((SEQ, D_MODEL), F32),
        ],
        compiler_params=pltpu.CompilerParams(
            dimension_semantics=("arbitrary",), vmem_limit_bytes=VMEM_LIMIT),
        name="inproj",
    )(x, w_in_b, ng, nb, ws_b, bs_full, wsg_b)


def _natten_kernel(q_ref, kp_ref, kc_ref, kn_ref, vp_ref, vc_ref, vn_ref, bias_ref,
                   o_ref, kwin_ref, vwin_ref):
    i = pl.program_id(0)
    kwin_ref[0:NA_ROWS] = kp_ref[...]
    kwin_ref[NA_ROWS:2 * NA_ROWS] = kc_ref[...]
    kwin_ref[2 * NA_ROWS:3 * NA_ROWS] = kn_ref[...]
    vwin_ref[0:NA_ROWS] = vp_ref[...]
    vwin_ref[NA_ROWS:2 * NA_ROWS] = vc_ref[...]
    vwin_ref[2 * NA_ROWS:3 * NA_ROWS] = vn_ref[...]

    n_keys = WIN_H * GRID_W
    group_heads = NA_GROUP_WIDTH // NA_HEAD_DIM
    n_groups = NA_WIDTH // NA_GROUP_WIDTH
    group_rows = group_heads * GRID_W
    lane_head = lax.broadcasted_iota(jnp.int32, (GRID_W, NA_GROUP_WIDTH), 1) // NA_HEAD_DIM
    for j in range(NA_ROWS):
        r = i * NA_ROWS + j
        rs = jnp.clip(r - WIN_H // 2, 0, GRID_ROWS - WIN_H)
        off = rs - (i - 1) * NA_ROWS
        dneg = r - rs
        kw = kwin_ref[pl.ds(off, WIN_H)].reshape(n_keys, NA_WIDTH)
        vw = vwin_ref[pl.ds(off, WIN_H)].reshape(n_keys, NA_WIDTH)
        qj = q_ref[j]
        outs = []
        for g in range(n_groups):
            chans = slice(g * NA_GROUP_WIDTH, (g + 1) * NA_GROUP_WIDTH)
            qg = qj[:, chans]
            qm = jnp.concatenate(
                [jnp.where(lane_head == h, qg, jnp.zeros_like(qg)) for h in range(group_heads)],
                axis=0)
            logits = lax.dot_general(qm, kw[:, chans], (((1,), (1,)), ((), ())),
                                     preferred_element_type=F32)
            logits = logits + bias_ref[dneg, g * group_rows:(g + 1) * group_rows, :]
            m = jnp.max(logits, axis=-1, keepdims=True)
            p = jnp.exp(logits - m)
            inv = 1.0 / jnp.sum(p, axis=-1, keepdims=True)
            r = jnp.dot(p.astype(BF16), vw[:, chans], preferred_element_type=F32) * inv
            acc = jnp.zeros((GRID_W, NA_GROUP_WIDTH), F32)
            for h in range(group_heads):
                acc = acc + jnp.where(lane_head == h, r[h * GRID_W:(h + 1) * GRID_W, :], 0.0)
            outs.append(acc.astype(BF16))
        o_ref[j] = jnp.concatenate(outs, axis=1)


def _natten(q, k, v, bias_tab):
    n_steps = GRID_ROWS // NA_ROWS
    q3 = q.reshape(GRID_ROWS, GRID_W, NA_WIDTH)
    k3 = k.reshape(GRID_ROWS, GRID_W, NA_WIDTH)
    v3 = v.reshape(GRID_ROWS, GRID_W, NA_WIDTH)
    blk = (NA_ROWS, GRID_W, NA_WIDTH)
    cur = pl.BlockSpec(blk, lambda i: (i, 0, 0))
    prev = pl.BlockSpec(blk, lambda i: (jnp.maximum(i - 1, 0), 0, 0))
    nxt = pl.BlockSpec(blk, lambda i: (jnp.minimum(i + 1, n_steps - 1), 0, 0))
    out = pl.pallas_call(
        _natten_kernel,
        grid=(n_steps,),
        in_specs=[cur, prev, cur, nxt, prev, cur, nxt,
                  pl.BlockSpec(bias_tab.shape, lambda i: (0, 0, 0))],
        out_specs=cur,
        out_shape=jax.ShapeDtypeStruct((GRID_ROWS, GRID_W, NA_WIDTH), BF16),
        scratch_shapes=[pltpu.VMEM((3 * NA_ROWS, GRID_W, NA_WIDTH), BF16),
                        pltpu.VMEM((3 * NA_ROWS, GRID_W, NA_WIDTH), BF16)],
        compiler_params=pltpu.CompilerParams(
            dimension_semantics=("arbitrary",), vmem_limit_bytes=VMEM_LIMIT),
        name="natten",
    )(q3, k3, k3, k3, v3, v3, v3, bias_tab)
    return out.reshape(SEQ, NA_WIDTH)


def _natten_bias_table(rpb):
    cols = np.arange(GRID_W)
    col_start = np.clip(cols - WIN_W // 2, 0, GRID_W - WIN_W)
    kc = np.arange(GRID_W)
    valid = (kc[None, :] >= col_start[:, None]) & (kc[None, :] < col_start[:, None] + WIN_W)
    dc = np.where(valid, kc[None, :] - cols[:, None] + (WIN_W - 1), -1)
    dr = np.arange(WIN_H)[None, :] + (WIN_H - 1) - np.arange(WIN_H)[:, None]
    rows = jnp.transpose(rpb[:, dr], (1, 0, 2, 3)).astype(F32)
    onehot = (dc[None, :, :] == np.arange(2 * WIN_W - 1)[:, None, None]).astype(np.float32)
    table = jnp.einsum('xhkd,dcz->xhckz', rows, jnp.asarray(onehot),
                       precision=lax.Precision.HIGHEST)
    outside = np.where(valid, 0.0, MASK_BIAS).astype(np.float32)
    table = table + jnp.asarray(outside)[None, None, :, None, :]
    return table.reshape(WIN_H, NA_HEADS * GRID_W, WIN_H * GRID_W)


def _mix_kernel(x_ref, yna_ref, ga_ref, sg_ref, wna_ref, wout_ref, g1_ref, b1_ref,
                x1T_ref, x1Tb_ref):
    merged = ga_ref[...] * jnp.dot(yna_ref[...], wna_ref[...], preferred_element_type=F32)
    merged = (merged + sg_ref[...]).astype(BF16)
    z = DN_ALPHA * x_ref[...] + jnp.dot(merged, wout_ref[...], preferred_element_type=F32)
    x1T = _layer_norm_rows(z, g1_ref[...], b1_ref[...]).T
    x1T_ref[...] = x1T
    x1Tb_ref[...] = x1T.astype(BF16)


def _mix(x, y_na, gate_a, sg_part, wna_b, wout_b, g1, b1):
    n_steps = SEQ // MIX_TM
    tok = lambda width: pl.BlockSpec((MIX_TM, width), lambda i: (i, 0))
    full = lambda shape: pl.BlockSpec(shape, lambda i: (0,) * len(shape))
    featT = pl.BlockSpec((D_MODEL, MIX_TM), lambda i: (0, i))
    return pl.pallas_call(
        _mix_kernel,
        grid=(n_steps,),
        in_specs=[
            tok(D_MODEL), tok(NA_WIDTH), tok(D_MODEL), tok(D_MODEL),
            full((NA_WIDTH, D_MODEL)), full((D_MODEL, D_MODEL)),
            full((1, D_MODEL)), full((1, D_MODEL)),
        ],
        out_specs=[featT, featT],
        out_shape=[
            jax.ShapeDtypeStruct((D_MODEL, SEQ), F32),
            jax.ShapeDtypeStruct((D_MODEL, SEQ), BF16),
        ],
        compiler_params=pltpu.CompilerParams(
            dimension_semantics=("arbitrary",), vmem_limit_bytes=VMEM_LIMIT),
        name="mix",
    )(x, y_na, gate_a, sg_part, wna_b, wout_b, g1, b1)


def _cmpx(xs, i, j):
    hi = jnp.maximum(xs[i], xs[j])
    lo = jnp.minimum(xs[i], xs[j])
    xs[i], xs[j] = hi, lo


def _bitonic_sort_desc(xs):
    xs = list(xs)
    n = len(xs)
    size = 2
    while size <= n:
        stride = size // 2
        while stride >= 1:
            for i in range(n):
                l = i ^ stride
                if l > i:
                    if (i & size) == 0:
                        _cmpx(xs, i, l)
                    else:
                        _cmpx(xs, l, i)
            stride //= 2
        size *= 2
    return xs


def _bitonic_merge_desc(xs):
    xs = list(xs)
    n = len(xs)
    stride = n // 2
    while stride >= 1:
        for i in range(n):
            if (i & stride) == 0:
                _cmpx(xs, i, i + stride)
        stride //= 2
    return xs


def _top_merge(xs, ys):
    n = len(xs)
    return _bitonic_merge_desc([jnp.maximum(xs[i], ys[n - 1 - i]) for i in range(n)])


def _top16_of(vals):
    k = PEER_TOPK
    pad = (-len(vals)) % k
    vals = list(vals) + [jnp.full_like(vals[0], -jnp.inf)] * pad
    groups = [_bitonic_sort_desc(vals[g:g + k]) for g in range(0, len(vals), k)]
    top = groups[0]
    for grp in groups[1:]:
        top = _top_merge(top, grp)
    return top


def _route_kernel(x1Tb_ref, wqT_ref, k1_ref, k2_ref,
                  thr_ref, c1_ref, v2_ref, s2_ref, s1k_ref, s2k_ref):
    qpT = jnp.dot(wqT_ref[...], x1Tb_ref[...], preferred_element_type=F32).astype(BF16)
    for h in range(PEER_HEADS):
        lo = h * PEER_KEY_DIM
        s1 = jnp.dot(k1_ref[...], qpT[lo:lo + PEER_HALF, :], preferred_element_type=F32)
        s2 = jnp.dot(k2_ref[...], qpT[lo + PEER_HALF:lo + PEER_KEY_DIM, :],
                     preferred_element_type=F32)
        s2_ref[h] = s2
        for slab in range(ROUTE_TT // LANES):
            lanes = slice(slab * LANES, (slab + 1) * LANES)
            key_rows = pl.ds(h, PEER_N_KEYS, stride=PEER_HEADS)
            s1k_ref[slab, key_rows, :] = s1[:, lanes]
            s2k_ref[slab, key_rows, :] = s2[:, lanes]

    def route_slab(slab, carry):
        def key(ref, a):
            return ref[slab, SUBLANES * a:SUBLANES * (a + 1), :]

        v1 = _top16_of([key(s1k_ref, a) for a in range(PEER_N_KEYS)])
        v2 = _top16_of([key(s2k_ref, b) for b in range(PEER_N_KEYS)])
        cands = [v1[i] + v2[j] for i in range(PEER_TOPK) for j in range(PEER_TOPK)
                 if (i + 1) * (j + 1) <= PEER_TOPK]
        top = _top16_of(cands)
        tau = top[PEER_TOPK - 1]
        z = jnp.ones_like(tau)
        for t in top[1:]:
            z = z + jnp.exp(t - top[0])
        half_inv_z = 0.5 / z
        need = []
        for j in range(PEER_TOPK):
            v2_ref[slab, SUBLANES * j:SUBLANES * (j + 1), :] = v2[j]
            nj = jnp.full((SUBLANES, LANES), jnp.inf, F32)
            for i in range(PEER_TOPK):
                nj = jnp.where(v1[i] + v2[j] >= tau, v1[i], nj)
            need.append(nj)
        p2_top = [jnp.exp(v2[j] - v2[0]) for j in range(PEER_TOPK)]
        for a in range(PEER_N_KEYS):
            s1a = key(s1k_ref, a)
            thr = jnp.full((SUBLANES, LANES), PEER_NO_MATCH, F32)
            for j in range(PEER_TOPK):
                thr = jnp.where(s1a >= need[j], p2_top[j], thr)
            thr_ref[slab, SUBLANES * a:SUBLANES * (a + 1), :] = thr
            c1_ref[slab, SUBLANES * a:SUBLANES * (a + 1), :] = jnp.exp(s1a - v1[0]) * half_inv_z
        return carry

    lax.fori_loop(0, ROUTE_TT // LANES, route_slab, 0)


def _route(x1Tb, wqT_b, k1_b, k2_b):
    n_steps = SEQ // ROUTE_TT
    slabs = ROUTE_TT // LANES
    key_rows = PEER_N_KEYS * PEER_HEADS
    top_rows = PEER_TOPK * PEER_HEADS
    full = lambda shape: pl.BlockSpec(shape, lambda i: (0,) * len(shape))
    key_major = pl.BlockSpec((slabs, key_rows, LANES), lambda i: (i, 0, 0))
    key_major_shape = jax.ShapeDtypeStruct((SEQ // LANES, key_rows, LANES), F32)
    return pl.pallas_call(
        _route_kernel,
        grid=(n_steps,),
        in_specs=[
            pl.BlockSpec((D_MODEL, ROUTE_TT), lambda i: (0, i)),
            full((PEER_HEADS * PEER_KEY_DIM, D_MODEL)),
            full((PEER_N_KEYS, PEER_HALF)), full((PEER_N_KEYS, PEER_HALF)),
        ],
        out_specs=[
            key_major, key_major,
            pl.BlockSpec((slabs, top_rows, LANES), lambda i: (i, 0, 0)),
            pl.BlockSpec((PEER_HEADS, PEER_N_KEYS, ROUTE_TT), lambda i: (0, 0, i)),
        ],
        out_shape=[
            key_major_shape, key_major_shape,
            jax.ShapeDtypeStruct((SEQ // LANES, top_rows, LANES), F32),
            jax.ShapeDtypeStruct((PEER_HEADS, PEER_N_KEYS, SEQ), F32),
        ],
        scratch_shapes=[pltpu.VMEM((slabs, key_rows, LANES), F32),
                        pltpu.VMEM((slabs, key_rows, LANES), F32)],
        compiler_params=pltpu.CompilerParams(
            dimension_semantics=("arbitrary",), vmem_limit_bytes=VMEM_LIMIT),
        name="route",
    )(x1Tb, wqT_b, k1_b, k2_b)


def _peer_kernel(x1T_ref, x1Tb_ref, u_ref, vT_ref, s2_ref, v2_ref, thr_ref, c1_ref, g2_ref, b2_ref,
                 o_ref, acc_ref, p2_ref, a0_ref, a1_ref, w0_ref, w1_ref):
    e = pl.program_id(1)

    @pl.when(e == 0)
    def _():
        acc_ref[...] = jnp.zeros_like(acc_ref)
        for h in range(PEER_HEADS):
            for lt in range(PEER_TT // LANES):
                lanes = slice(lt * LANES, (lt + 1) * LANES)
                p2_ref[h, :, lanes] = jnp.exp(s2_ref[h, :, lanes] - v2_ref[lt, h:h + 1, :])

    blocks_per_tile = PEER_ET // PEER_N_KEYS
    n_groups = PEER_ET // PEER_GROUP
    blocks_per_group = PEER_GROUP // PEER_N_KEYS
    a_bufs = (a0_ref, a1_ref)
    w_bufs = (w0_ref, w1_ref)

    def scores(grp):
        rows = slice(grp * PEER_GROUP, (grp + 1) * PEER_GROUP)
        return jnp.dot(u_ref[rows, :], x1Tb_ref[...], preferred_element_type=F32)

    def contract(grp):
        cols = slice(grp * PEER_GROUP, (grp + 1) * PEER_GROUP)
        acc_ref[...] += jnp.dot(vT_ref[0, :, cols], w_bufs[grp % 2][...],
                                preferred_element_type=F32)

    def gate(grp):
        a_ref, w_ref = a_bufs[grp % 2], w_bufs[grp % 2]
        for blk in range(blocks_per_group):
            a = e * blocks_per_tile + grp * blocks_per_group + blk
            head_rows = pl.ds(pl.multiple_of(a * PEER_HEADS, PEER_HEADS), PEER_HEADS)
            for lt in range(PEER_TT // LANES):
                lanes = slice(lt * LANES, (lt + 1) * LANES)
                thr_heads = thr_ref[lt, head_rows, :]
                c1_heads = c1_ref[lt, head_rows, :]
                for rc in range(PEER_N_KEYS // PEER_CHUNK_ROWS):
                    rows = slice(rc * PEER_CHUNK_ROWS, (rc + 1) * PEER_CHUNK_ROWS)
                    g = None
                    for h in range(PEER_HEADS):
                        thr = thr_heads[h:h + 1, :]
                        c1 = c1_heads[h:h + 1, :]
                        p2 = p2_ref[h, rows, lanes]
                        term = jnp.where(p2 >= thr, p2 * c1, 0.0)
                        g = term if g is None else g + term
                    out_rows = slice(blk * PEER_N_KEYS + rc * PEER_CHUNK_ROWS,
                                     blk * PEER_N_KEYS + (rc + 1) * PEER_CHUNK_ROWS)
                    w_ref[out_rows, lanes] = _gelu_times(a_ref[out_rows, lanes], g)

    a_bufs[0][...] = scores(0)
    for grp in range(n_groups):
        if grp + 1 < n_groups:
            a_bufs[(grp + 1) % 2][...] = scores(grp + 1)
        gate(grp)
        if grp > 0:
            contract(grp - 1)
    contract(n_groups - 1)

    @pl.when(e == pl.num_programs(1) - 1)
    def _():
        z = DN_ALPHA * x1T_ref[...] + acc_ref[...]
        y = _layer_norm_cols(z, _tile_lanes(g2_ref[...], PEER_TT), _tile_lanes(b2_ref[...], PEER_TT))
        o_ref[...] = y.T


def _peer(x1T, x1Tb, u_b, vT_b, s2T, v2, thr, c1, g2, b2):
    n_tok = SEQ // PEER_TT
    n_exp = PEER_N_EXPERTS // PEER_ET
    featT = pl.BlockSpec((D_MODEL, PEER_TT), lambda j, e: (0, j))
    side = pl.BlockSpec((PEER_HEADS, PEER_N_KEYS, PEER_TT), lambda j, e: (0, 0, j))
    key_major = pl.BlockSpec((PEER_TT // LANES, PEER_N_KEYS * PEER_HEADS, LANES),
                             lambda j, e: (j, 0, 0))
    par = pl.BlockSpec((D_MODEL, LANES), lambda j, e: (0, 0))
    return pl.pallas_call(
        _peer_kernel,
        grid=(n_tok, n_exp),
        in_specs=[
            featT, featT,
            pl.BlockSpec((PEER_ET, D_MODEL), lambda j, e: (e, 0)),
            pl.BlockSpec((1, D_MODEL, PEER_ET), lambda j, e: (e, 0, 0)),
            side,
            pl.BlockSpec((PEER_TT // LANES, PEER_TOPK * PEER_HEADS, LANES), lambda j, e: (j, 0, 0)),
            key_major, key_major, par, par,
        ],
        out_specs=pl.BlockSpec((PEER_TT, D_MODEL), lambda j, e: (j, 0)),
        out_shape=jax.ShapeDtypeStruct((SEQ, D_MODEL), F32),
        scratch_shapes=[
            pltpu.VMEM((D_MODEL, PEER_TT), F32),
            pltpu.VMEM((PEER_HEADS, PEER_N_KEYS, PEER_TT), F32),
            pltpu.VMEM((PEER_GROUP, PEER_TT), F32),
            pltpu.VMEM((PEER_GROUP, PEER_TT), F32),
            pltpu.VMEM((PEER_GROUP, PEER_TT), BF16),
            pltpu.VMEM((PEER_GROUP, PEER_TT), BF16),
        ],
        compiler_params=pltpu.CompilerParams(
            dimension_semantics=("arbitrary", "arbitrary"), vmem_limit_bytes=VMEM_LIMIT),
        name="peer",
    )(x1T, x1Tb, u_b, vT_b, s2T, v2, thr, c1, g2, b2)


def _lane_replicated(p):
    return jnp.broadcast_to(p.astype(F32)[:, None], (p.shape[0], LANES))


def _transpose_cast_kernel(x_ref, o_ref):
    o_ref[0] = x_ref[...].T.astype(BF16)


def _transpose_cast(w, tile):
    rows, cols = w.shape
    per_tile = tile // XPOSE_ROWS
    return pl.pallas_call(
        _transpose_cast_kernel,
        grid=(rows // XPOSE_ROWS,),
        in_specs=[pl.BlockSpec((XPOSE_ROWS, cols), lambda i: (i, 0))],
        out_specs=pl.BlockSpec((1, cols, XPOSE_ROWS), lambda i: (i // per_tile, 0, i % per_tile)),
        out_shape=jax.ShapeDtypeStruct((rows // tile, cols, tile), BF16),
        compiler_params=pltpu.CompilerParams(
            dimension_semantics=("arbitrary",), vmem_limit_bytes=VMEM_LIMIT),
        name="transpose_cast",
    )(w)


def kernel(x, w_in, na_rpb, sg_norm_g, sg_norm_b, sg_spatial_w, sg_spatial_b, w_branch_na,
           w_branch_sg, w_out, ln1_g, ln1_b, peer_wq, peer_subkeys1, peer_subkeys2, peer_u, peer_v,
           ln2_g, ln2_b):
    assert x.shape == (1, SEQ, D_MODEL) and w_in.shape[0] == 1
    l = 0
    xs = x[0]
    q, k, v, gate_a, sg_part = _inproj(
        xs,
        w_in[l].astype(BF16),
        sg_norm_g[l][None, :], sg_norm_b[l][None, :],
        sg_spatial_w[l].astype(BF16),
        jnp.repeat(sg_spatial_b[l].T, SG_GROUP_DIM, axis=1),
        w_branch_sg[l].astype(BF16),
    )
    y_na = _natten(q, k, v, _natten_bias_table(na_rpb[l]))
    x1T, x1Tb = _mix(
        xs, y_na, gate_a, sg_part,
        w_branch_na[l].astype(BF16),
        w_out[l].astype(BF16),
        ln1_g[l][None, :], ln1_b[l][None, :],
    )
    thr, c1, v2, s2T = _route(
        x1Tb, _transpose_cast(peer_wq[l], D_MODEL)[0],
        peer_subkeys1[l].astype(BF16), peer_subkeys2[l].astype(BF16),
    )
    out = _peer(
        x1T, x1Tb,
        peer_u[l].astype(BF16), _transpose_cast(peer_v[l], PEER_ET),
        s2T, v2, thr, c1,
        _lane_replicated(ln2_g[l]), _lane_replicated(ln2_b[l]),
    )
    return out[None]
```
